```python
import jax, jax.numpy as jnp
from jax import lax
import numpy as np

D_MODEL = 2048
BATCH = 4
SEQ = 2048
DEPTH = 2
DEC_BATCH = 128
DEC_SEQ = 8
PAST_LEN = 16384
PAGE_SIZE = 128

N_MIXERS = 2
N_LAYERS_A = (DEPTH + 1) // 2
N_LAYERS_B = DEPTH // 2
CONV_A_WIDTH = 3
GDN_HEAD_DIM = 128
GDN_QK_HEADS = D_MODEL // GDN_HEAD_DIM
GDN_V_HEADS = 2 * GDN_QK_HEADS
GDN_KEY_DIM = GDN_QK_HEADS * GDN_HEAD_DIM
GDN_VAL_DIM = GDN_V_HEADS * GDN_HEAD_DIM
GDN_CONV_DIM = 2 * GDN_KEY_DIM + GDN_VAL_DIM
GDN_PROJ_DIM = GDN_CONV_DIM + GDN_VAL_DIM + 2 * GDN_V_HEADS
CONV_B_WIDTH = 4
GDN_CHUNK = 64
D_FF = 4 * D_MODEL
EPS = 1e-6

kernel_name = "hybrid_shortconv_gdn_decode_step"


def _rmsnorm(x, w):
    xf = x.astype(jnp.float32)
    y = xf * lax.rsqrt(jnp.mean(xf * xf, axis=-1, keepdims=True) + EPS)
    return (y * w.astype(jnp.float32)).astype(x.dtype)


def _l2norm(x):
    return x * lax.rsqrt(jnp.sum(x * x, axis=-1, keepdims=True) + EPS)


def _causal_dwconv(xp, w, length):
    width = w.shape[0]
    y = xp[:, 0:length] * w[0]
    for j in range(1, width):
        y = y + xp[:, j:j + length] * w[j]
    return y


def _sqrelu_mlp(h, w_up, w_down):
    return jnp.square(jax.nn.relu(h @ w_up)) @ w_down


def _short_conv_mixer(h, buf, w_in, w_conv, w_out):
    gate_b, gate_c, xv = jnp.split(h @ w_in, 3, axis=-1)
    u = gate_c * xv
    up = jnp.concatenate([buf.astype(u.dtype), u], axis=1)
    y = _causal_dwconv(up, w_conv, h.shape[1])
    return (gate_b * y) @ w_out, up[:, -(CONV_A_WIDTH - 1):]


def _to_chunks(t, n, c):
    bsz = t.shape[0]
    t = t.reshape((bsz, n, c) + t.shape[2:])
    return jnp.transpose(t, (1, 0, 3, 2) + tuple(range(4, t.ndim)))


def _gated_delta_chunked(q, k, v, g, beta, s0):
    bsz, length, heads, dk = q.shape
    dv = v.shape[-1]
    c = min(GDN_CHUNK, length)
    n = -(-length // c)
    pad = n * c - length

    def prep(t):
        t = jnp.pad(t, [(0, 0), (0, pad)] + [(0, 0)] * (t.ndim - 2))
        return _to_chunks(t, n, c)

    qc, kc, vc, gc, bc = prep(q), prep(k), prep(v), prep(g), prep(beta)
    G = jnp.cumsum(gc, axis=-1)
    idx = jnp.arange(c)
    tril = idx[:, None] >= idx[None, :]
    strict = idx[:, None] > idx[None, :]
    decay = jnp.exp(jnp.where(tril, G[..., :, None] - G[..., None, :], -jnp.inf))
    kk = jnp.einsum('nbhid,nbhjd->nbhij', kc, kc)
    lmat = jnp.where(strict, bc[..., :, None] * kk * decay, 0.0)
    eye = jnp.eye(c, dtype=jnp.float32)
    rhs = jnp.concatenate([vc * bc[..., None], kc * (bc * jnp.exp(G))[..., None]], axis=-1)
    sol = lax.linalg.triangular_solve(eye + lmat, rhs, left_side=True, lower=True,
                                      unit_diagonal=True)
    u_c, w_c = sol[..., :dv], sol[..., dv:]
    qk = jnp.where(tril, jnp.einsum('nbhid,nbhjd->nbhij', qc, kc) * decay, 0.0)
    q_dec = qc * jnp.exp(G)[..., None]
    k_dec = kc * jnp.exp(G[..., -1:] - G)[..., None]
    g_last = jnp.exp(G[..., -1])

    def step(S, xs):
        qk_i, qd_i, kd_i, u_i, w_i, gl_i = xs
        v_new = u_i - jnp.einsum('bhcd,bhde->bhce', w_i, S)
        o_i = (jnp.einsum('bhcd,bhde->bhce', qd_i, S)
               + jnp.einsum('bhij,bhje->bhie', qk_i, v_new))
        S = S * gl_i[..., None, None] + jnp.einsum('bhcd,bhce->bhde', kd_i, v_new)
        return S, o_i

    s_fin, o = lax.scan(step, s0, (qk, q_dec, k_dec, u_c, w_c, g_last))
    o = jnp.transpose(o, (1, 0, 3, 2, 4)).reshape(bsz, n * c, heads, dv)[:, :length]
    return o, s_fin


def _gated_deltanet(h, buf, s0, w_in, w_conv, a_log, dt_bias, o_norm, w_out):
    bsz, length, _ = h.shape
    proj = h @ w_in
    o1 = GDN_CONV_DIM
    o2 = o1 + GDN_VAL_DIM
    o3 = o2 + GDN_V_HEADS
    qkv, z, b_raw, a_raw = proj[..., :o1], proj[..., o1:o2], proj[..., o2:o3], proj[..., o3:]
    qkv_p = jnp.concatenate([buf.astype(qkv.dtype), qkv], axis=1)
    qkv_c = jax.nn.silu(_causal_dwconv(qkv_p, w_conv, length)).astype(jnp.float32)
    q = qkv_c[..., :GDN_KEY_DIM].reshape(bsz, length, GDN_QK_HEADS, GDN_HEAD_DIM)
    k = qkv_c[..., GDN_KEY_DIM:2 * GDN_KEY_DIM].reshape(bsz, length, GDN_QK_HEADS, GDN_HEAD_DIM)
    v = qkv_c[..., 2 * GDN_KEY_DIM:].reshape(bsz, length, GDN_V_HEADS, GDN_HEAD_DIM)
    rep = GDN_V_HEADS // GDN_QK_HEADS
    q = jnp.repeat(_l2norm(q), rep, axis=2) * (GDN_HEAD_DIM ** -0.5)
    k = jnp.repeat(_l2norm(k), rep, axis=2)
    beta = jax.nn.sigmoid(b_raw.astype(jnp.float32))
    g = -jnp.exp(a_log.astype(jnp.float32)) * jax.nn.softplus(
        a_raw.astype(jnp.float32) + dt_bias.astype(jnp.float32))
    o, s_new = _gated_delta_chunked(q, k, v, g, beta, s0.astype(jnp.float32))
    gate = jax.nn.silu(z.astype(jnp.float32)).reshape(bsz, length, GDN_V_HEADS, GDN_HEAD_DIM)
    o = o * lax.rsqrt(jnp.mean(o * o, axis=-1, keepdims=True) + EPS) * o_norm.astype(jnp.float32) * gate
    out = o.reshape(bsz, length, GDN_VAL_DIM).astype(h.dtype) @ w_out
    return out, qkv_p[:, -(CONV_B_WIDTH - 1):], s_new.astype(h.dtype)


def _trunk(x, conv_a, conv_b, ssm_b, norm_a, w_in_a, w_conv_a, w_out_a, norm_b, w_in_b,
           w_conv_b, a_log_b, dt_bias_b, o_norm_b, w_out_b, mlp_norm, w_up, w_down, final_norm):
    new_a, new_cb, new_s = [], [], []
    for i in range(DEPTH):
        j = i // N_MIXERS
        if i % N_MIXERS == 0:
            mix, buf = _short_conv_mixer(_rmsnorm(x, norm_a[j]), conv_a[j],
                                         w_in_a[j], w_conv_a[j], w_out_a[j])
            new_a.append(buf)
        else:
            mix, buf, s = _gated_deltanet(_rmsnorm(x, norm_b[j]), conv_b[j], ssm_b[j], w_in_b[j],
                                          w_conv_b[j], a_log_b[j], dt_bias_b[j], o_norm_b[j],
                                          w_out_b[j])
            new_cb.append(buf)
            new_s.append(s)
        x = x + mix
        x = x + _sqrelu_mlp(_rmsnorm(x, mlp_norm[i]), w_up[i], w_down[i])
    return _rmsnorm(x, final_norm), jnp.stack(new_a), jnp.stack(new_cb), jnp.stack(new_s)


def setup_inputs(seed: int = 0) -> dict:
    key = jax.random.key(seed)
    ks = jax.random.split(key, 21)

    def nrm(k, shape, scale):
        return jax.random.normal(k, shape, jnp.float32) * scale

    def gain(k, shape):
        return 1.0 + 0.02 * jax.random.normal(k, shape, jnp.float32)

    dt = jnp.exp(jax.random.uniform(ks[13], (N_LAYERS_B, GDN_V_HEADS), jnp.float32,
                                    minval=float(np.log(1e-3)), maxval=float(np.log(1e-1))))
    dt_bias = dt + jnp.log(-jnp.expm1(-dt))
    return {
        "x_prompt": nrm(ks[0], (BATCH, SEQ, D_MODEL), 1.0),
        "x_sample": nrm(ks[1], (DEC_BATCH, DEC_SEQ, D_MODEL), 1.0),
        "state_conv_a": nrm(ks[2], (N_LAYERS_A, DEC_BATCH, CONV_A_WIDTH - 1, D_MODEL), 1.0),
        "state_conv_b": nrm(ks[3], (N_LAYERS_B, DEC_BATCH, CONV_B_WIDTH - 1, GDN_CONV_DIM), 1.0),
        "state_ssm_b": nrm(ks[4], (N_LAYERS_B, DEC_BATCH, GDN_V_HEADS, GDN_HEAD_DIM, GDN_HEAD_DIM), 0.05),
        "norm_a": gain(ks[5], (N_LAYERS_A, D_MODEL)),
        "w_in_a": nrm(ks[6], (N_LAYERS_A, D_MODEL, 3 * D_MODEL), D_MODEL ** -0.5),
        "w_conv_a": nrm(ks[7], (N_LAYERS_A, CONV_A_WIDTH, D_MODEL), CONV_A_WIDTH ** -0.5),
        "w_out_a": nrm(ks[8], (N_LAYERS_A, D_MODEL, D_MODEL), D_MODEL ** -0.5),
        "norm_b": gain(ks[9], (N_LAYERS_B, D_MODEL)),
        "w_in_b": nrm(ks[10], (N_LAYERS_B, D_MODEL, GDN_PROJ_DIM), D_MODEL ** -0.5),
        "w_conv_b": nrm(ks[11], (N_LAYERS_B, CONV_B_WIDTH, GDN_CONV_DIM), CONV_B_WIDTH ** -0.5),
        "a_log_b": jnp.log(jax.random.uniform(ks[12], (N_LAYERS_B, GDN_V_HEADS), jnp.float32,
                                               minval=1.0, maxval=16.0)),
        "dt_bias_b": dt_bias,
        "o_norm_b": gain(ks[14], (N_LAYERS_B, GDN_HEAD_DIM)),
        "w_out_b": nrm(ks[15], (N_LAYERS_B, GDN_VAL_DIM, D_MODEL), GDN_VAL_DIM ** -0.5),
        "mlp_norm": gain(ks[16], (DEPTH, D_MODEL)),
        "w_up": nrm(ks[17], (DEPTH, D_MODEL, D_FF), D_MODEL ** -0.5),
        "w_down": nrm(ks[18], (DEPTH, D_FF, D_MODEL), D_FF ** -0.5),
        "final_norm": gain(ks[19], (D_MODEL,)),
    }


def reference(x_prompt, x_sample, state_conv_a, state_conv_b, state_ssm_b, norm_a, w_in_a,
              w_conv_a, w_out_a, norm_b, w_in_b, w_conv_b, a_log_b, dt_bias_b, o_norm_b,
              w_out_b, mlp_norm, w_up, w_down, final_norm):
    bp = x_prompt.shape[0]
    zero_a = jnp.zeros((N_LAYERS_A, bp, CONV_A_WIDTH - 1, D_MODEL), x_prompt.dtype)
    zero_cb = jnp.zeros((N_LAYERS_B, bp, CONV_B_WIDTH - 1, GDN_CONV_DIM), x_prompt.dtype)
    zero_s = jnp.zeros((N_LAYERS_B, bp, GDN_V_HEADS, GDN_HEAD_DIM, GDN_HEAD_DIM), jnp.float32)
    y_prompt, conv_a_p, conv_b_p, ssm_b_p = _trunk(
        x_prompt, zero_a, zero_cb, zero_s, norm_a, w_in_a, w_conv_a, w_out_a, norm_b, w_in_b,
        w_conv_b, a_log_b, dt_bias_b, o_norm_b, w_out_b, mlp_norm, w_up, w_down, final_norm)
    y_sample, conv_a_s, conv_b_s, ssm_b_s = _trunk(
        x_sample, state_conv_a, state_conv_b, state_ssm_b, norm_a, w_in_a, w_conv_a, w_out_a,
        norm_b, w_in_b, w_conv_b, a_log_b, dt_bias_b, o_norm_b, w_out_b, mlp_norm, w_up,
        w_down, final_norm)
    return (y_prompt, y_sample, conv_a_p, conv_b_p, ssm_b_p, conv_a_s, conv_b_s, ssm_b_s)
```

```python
import functools

import jax
import jax.numpy as jnp
from jax import lax
from jax.experimental import pallas as pl
from jax.experimental.pallas import tpu as pltpu

D_MODEL = 2048
D_FF = 4 * D_MODEL
HEAD_DIM = 128
QK_HEADS = D_MODEL // HEAD_DIM
V_HEADS = 2 * QK_HEADS
KEY_DIM = QK_HEADS * HEAD_DIM
VAL_DIM = V_HEADS * HEAD_DIM
CONV_DIM = 2 * KEY_DIM + VAL_DIM
QKVZ_DIM = CONV_DIM + VAL_DIM
CONV_A_WIDTH = 3
CONV_B_WIDTH = 4
PROMPT_CHUNK = 64
EPS = 1e-6

F32 = jnp.float32
BF16 = jnp.bfloat16

SUBLANES = 8
ROW_TILE = 512
GDN_ROWS = 256
VMEM_LIMIT = 56 * 1024 * 1024


def _params(semantics):
    return pltpu.CompilerParams(dimension_semantics=semantics, vmem_limit_bytes=VMEM_LIMIT)


def _rmsnorm(x, w):
    ms = jnp.mean(x * x, axis=-1, keepdims=True)
    return (x * lax.rsqrt(ms + EPS)) * w


def _dot(a, b):
    return jnp.dot(a, b, preferred_element_type=F32)


def _dot_nt(a, b):
    return lax.dot_general(a, b, (((1,), (1,)), ((), ())), preferred_element_type=F32)


def _dot_tn(a, b):
    return lax.dot_general(a, b, (((0,), (0,)), ((), ())), preferred_element_type=F32)


def _split(x):
    hi = x.astype(BF16)
    return hi, (x - hi.astype(F32)).astype(BF16)


def _dot_split(a, b):
    return _dot(a[0], b[0]) + (_dot(a[0], b[1]) + _dot(a[1], b[0]))


def _sigmoid(x):
    return 1.0 / (1.0 + jnp.exp(-x))


def _softplus(x):
    return jnp.maximum(x, 0.0) + jnp.log1p(jnp.exp(-jnp.abs(x)))


def _prev_rows_seq(u, halo, j):
    tm = u.shape[0]
    return pltpu.roll(jnp.concatenate([u, halo], axis=0), j, 0)[0:tm]


def _prev_rows_grouped(u, state, j, width):
    tm = u.shape[0]
    rmod = lax.broadcasted_iota(jnp.int32, u.shape, 0) & (SUBLANES - 1)
    back = width - 1 - j
    st = state if back == 0 else pltpu.roll(state, tm - back, 0)
    return jnp.where(rmod < j, st, pltpu.roll(u, j, 0))


def _mixer_a_kernel(x_ref, nw_ref, wb_ref, wc_ref, wx_ref, cw_ref, st_ref, u_ref, y_ref,
                    hs_ref, carry_ref, *, tiles_per_seq, grouped):
    i = pl.program_id(0)
    j = pl.program_id(1)

    @pl.when(j == 0)
    def _():
        hs_ref[...] = _rmsnorm(x_ref[...], nw_ref[...]).astype(BF16)

    hs = hs_ref[...]
    gate_b = _dot(hs, wb_ref[...])
    u = _dot(hs, wc_ref[...]) * _dot(hs, wx_ref[...])
    u_ref[...] = u
    tm = u.shape[0]
    if grouped:
        state = st_ref[...]
        prev1 = _prev_rows_grouped(u, state, 1, CONV_A_WIDTH)
        prev2 = _prev_rows_grouped(u, state, 2, CONV_A_WIDTH)
    else:
        halo = jnp.where(i % tiles_per_seq != 0, carry_ref[j], 0.0)
        prev1 = _prev_rows_seq(u, halo, 1)
        prev2 = _prev_rows_seq(u, halo, 2)
        carry_ref[j] = u[tm - SUBLANES:tm]
    y = prev2 * cw_ref[0:1, :] + prev1 * cw_ref[1:2, :] + u * cw_ref[2:3, :]
    y_ref[...] = (gate_b * y).astype(BF16)


def _mixer_a(x, norm_w, w_in, conv_w, state, *, seq_len, grouped):
    t = x.shape[0]
    tm, tn = ROW_TILE, 512
    nj = D_MODEL // tn
    if grouped:
        st_spec = pl.BlockSpec((tm, tn), lambda i, j: (i, j))
    else:
        st_spec = pl.BlockSpec((SUBLANES, tn), lambda i, j: (0, j))
    kern = functools.partial(_mixer_a_kernel, tiles_per_seq=max(seq_len // tm, 1), grouped=grouped)
    return pl.pallas_call(
        kern,
        grid=(t // tm, nj),
        in_specs=[
            pl.BlockSpec((tm, D_MODEL), lambda i, j: (i, 0)),
            pl.BlockSpec((1, D_MODEL), lambda i, j: (0, 0)),
            pl.BlockSpec((D_MODEL, tn), lambda i, j: (0, j)),
            pl.BlockSpec((D_MODEL, tn), lambda i, j: (0, nj + j)),
            pl.BlockSpec((D_MODEL, tn), lambda i, j: (0, 2 * nj + j)),
            pl.BlockSpec((CONV_A_WIDTH, tn), lambda i, j: (0, j)),
            st_spec,
        ],
        out_specs=[
            pl.BlockSpec((tm, tn), lambda i, j: (i, j)),
            pl.BlockSpec((tm, tn), lambda i, j: (i, j)),
        ],
        out_shape=[
            jax.ShapeDtypeStruct((t, D_MODEL), F32),
            jax.ShapeDtypeStruct((t, D_MODEL), BF16),
        ],
        scratch_shapes=[
            pltpu.VMEM((tm, D_MODEL), BF16),
            pltpu.VMEM((nj, SUBLANES, tn), F32),
        ],
        compiler_params=_params(("arbitrary", "arbitrary")),
        name="mixer_a",
    )(x, norm_w, w_in, w_in, w_in, conv_w, state)


def _matmul_residual_kernel(a_ref, w_ref, r_ref, o_ref):
    o_ref[...] = r_ref[...] + _dot(a_ref[...], w_ref[...])


def _matmul_residual(a, w, res):
    t, k = a.shape
    n = w.shape[1]
    tm, tn = ROW_TILE, 512
    return pl.pallas_call(
        _matmul_residual_kernel,
        grid=(t // tm, n // tn),
        in_specs=[
            pl.BlockSpec((tm, k), lambda i, j: (i, 0)),
            pl.BlockSpec((k, tn), lambda i, j: (0, j)),
            pl.BlockSpec((tm, tn), lambda i, j: (i, j)),
        ],
        out_specs=pl.BlockSpec((tm, tn), lambda i, j: (i, j)),
        out_shape=jax.ShapeDtypeStruct((t, n), F32),
        compiler_params=_params(("parallel", "arbitrary")),
        name="matmul_residual",
    )(a, w, res)


def _mlp_kernel(x_ref, nw_ref, wu_ref, wd_ref, fw_ref, o_ref, hs_ref, acc_ref, *, final_norm):
    j = pl.program_id(1)

    @pl.when(j == 0)
    def _():
        x = x_ref[...]
        hs_ref[...] = _rmsnorm(x, nw_ref[...]).astype(BF16)
        acc_ref[...] = x

    a = _dot(hs_ref[...], wu_ref[...])
    a = jnp.square(jnp.maximum(a, 0.0)).astype(BF16)
    acc_ref[...] += _dot(a, wd_ref[...])

    @pl.when(j == pl.num_programs(1) - 1)
    def _():
        r = acc_ref[...]
        o_ref[...] = _rmsnorm(r, fw_ref[...]) if final_norm else r


def _mlp(x, norm_w, w_up, w_down, final_w, *, final_norm):
    t = x.shape[0]
    tm, tf = ROW_TILE, 1024
    kern = functools.partial(_mlp_kernel, final_norm=final_norm)
    return pl.pallas_call(
        kern,
        grid=(t // tm, D_FF // tf),
        in_specs=[
            pl.BlockSpec((tm, D_MODEL), lambda i, j: (i, 0)),
            pl.BlockSpec((1, D_MODEL), lambda i, j: (0, 0)),
            pl.BlockSpec((D_MODEL, tf), lambda i, j: (0, j)),
            pl.BlockSpec((tf, D_MODEL), lambda i, j: (j, 0)),
            pl.BlockSpec((1, D_MODEL), lambda i, j: (0, 0)),
        ],
        out_specs=pl.BlockSpec((tm, D_MODEL), lambda i, j: (i, 0)),
        out_shape=jax.ShapeDtypeStruct((t, D_MODEL), F32),
        scratch_shapes=[
            pltpu.VMEM((tm, D_MODEL), BF16),
            pltpu.VMEM((tm, D_MODEL), F32),
        ],
        compiler_params=_params(("parallel", "arbitrary")),
        name="mlp",
    )(x, norm_w, w_up, w_down, final_w)


def _gdn_in_kernel(x_ref, nw_ref, w_ref, wg_ref, alog_ref, dtb_ref, qkvz_ref, gates_ref, hs_ref,
                   *, chunk):
    j = pl.program_id(1)

    @pl.when(j == 0)
    def _():
        hs = _rmsnorm(x_ref[...], nw_ref[...]).astype(BF16)
        hs_ref[...] = hs
        raw = _dot_nt(wg_ref[...], hs)
        tm = raw.shape[1]
        beta = _sigmoid(raw[0:V_HEADS])
        g = -jnp.exp(alog_ref[...]) * _softplus(raw[V_HEADS:2 * V_HEADS] + dtb_ref[...])
        pos = lax.broadcasted_iota(jnp.int32, g.shape, 1) & (chunk - 1)
        csum = g
        ssum = g
        s = 1
        while s < chunk:
            csum = csum + jnp.where(pos >= s, pltpu.roll(csum, s, 1), 0.0)
            ssum = ssum + jnp.where(pos < chunk - s, pltpu.roll(ssum, tm - s, 1), 0.0)
            s *= 2
        rest = ssum - g
        zero2 = jnp.zeros((2, tm), F32)
        for p in range(QK_HEADS):
            gates_ref[p, 0:2, :] = beta[2 * p:2 * p + 2]
            gates_ref[p, 2:4, :] = csum[2 * p:2 * p + 2]
            gates_ref[p, 4:6, :] = rest[2 * p:2 * p + 2]
            gates_ref[p, 6:8, :] = zero2

    qkvz_ref[...] = _dot(hs_ref[...], w_ref[...])


def _gdn_in(x, norm_w, w_qkvz, w_gates_t, a_log_col, dt_bias_col, *, chunk):
    t = x.shape[0]
    tm, tn = ROW_TILE, 1024
    kern = functools.partial(_gdn_in_kernel, chunk=chunk)
    return pl.pallas_call(
        kern,
        grid=(t // tm, QKVZ_DIM // tn),
        in_specs=[
            pl.BlockSpec((tm, D_MODEL), lambda i, j: (i, 0)),
            pl.BlockSpec((1, D_MODEL), lambda i, j: (0, 0)),
            pl.BlockSpec((D_MODEL, tn), lambda i, j: (0, j)),
            pl.BlockSpec((2 * V_HEADS, D_MODEL), lambda i, j: (0, 0)),
            pl.BlockSpec((V_HEADS, 1), lambda i, j: (0, 0)),
            pl.BlockSpec((V_HEADS, 1), lambda i, j: (0, 0)),
        ],
        out_specs=[
            pl.BlockSpec((tm, tn), lambda i, j: (i, j)),
            pl.BlockSpec((QK_HEADS, SUBLANES, tm), lambda i, j: (0, 0, i)),
        ],
        out_shape=[
            jax.ShapeDtypeStruct((t, QKVZ_DIM), F32),
            jax.ShapeDtypeStruct((QK_HEADS, SUBLANES, t), F32),
        ],
        scratch_shapes=[pltpu.VMEM((tm, D_MODEL), BF16)],
        compiler_params=_params(("parallel", "arbitrary")),
        name="gdn_in",
    )(x, norm_w, w_qkvz, w_gates_t, a_log_col, dt_bias_col)


def _gdn_conv_kernel(x_ref, st_ref, cw_ref, o_ref, *, tiles_per_seq, grouped, tn):
    i = pl.program_id(0)
    j = pl.program_id(1)
    u = x_ref[...]
    if grouped:
        state = st_ref[...]
        prev = [_prev_rows_grouped(u, state, s, CONV_B_WIDTH) for s in (1, 2, 3)]
    else:
        halo = jnp.where(i % tiles_per_seq != 0, st_ref[...], 0.0)
        prev = [_prev_rows_seq(u, halo, s) for s in (1, 2, 3)]
    y = prev[2] * cw_ref[0:1, :] + prev[1] * cw_ref[1:2, :] + prev[0] * cw_ref[2:3, :] + u * cw_ref[3:4, :]
    y = y * _sigmoid(y)
    n_qk = 2 * KEY_DIM // tn

    @pl.when(j >= n_qk)
    def _():
        o_ref[...] = y

    @pl.when(j < n_qk)
    def _():
        scale = jnp.where(j < KEY_DIM // tn, HEAD_DIM ** -0.5, 1.0).astype(F32)
        for hd in range(tn // HEAD_DIM):
            cols = slice(hd * HEAD_DIM, (hd + 1) * HEAD_DIM)
            yh = y[:, cols]
            ss = jnp.sum(yh * yh, axis=-1, keepdims=True)
            o_ref[:, cols] = (yh * lax.rsqrt(ss + EPS)) * scale


def _gdn_conv(qkvz, state, conv_w, *, seq_len, grouped):
    t = qkvz.shape[0]
    tm, tn = ROW_TILE, 1024
    if grouped:
        st_spec = pl.BlockSpec((tm, tn), lambda i, j: (i, j))
        st = state
    else:
        blocks_per_tile = tm // SUBLANES
        st_spec = pl.BlockSpec((SUBLANES, tn), lambda i, j: (jnp.maximum(i * blocks_per_tile - 1, 0), j))
        st = qkvz
    kern = functools.partial(_gdn_conv_kernel, tiles_per_seq=max(seq_len // tm, 1), grouped=grouped, tn=tn)
    return pl.pallas_call(
        kern,
        grid=(t // tm, CONV_DIM // tn),
        in_specs=[
            pl.BlockSpec((tm, tn), lambda i, j: (i, j)),
            st_spec,
            pl.BlockSpec((CONV_B_WIDTH, tn), lambda i, j: (0, j)),
        ],
        out_specs=pl.BlockSpec((tm, tn), lambda i, j: (i, j)),
        out_shape=jax.ShapeDtypeStruct((t, CONV_DIM), F32),
        compiler_params=_params(("parallel", "parallel")),
        name="gdn_conv",
    )(qkvz, st, conv_w)


def _gdn_chunk_parts(q, k, v, kk, qk, cols, grow, hh, chunk):
    rows = q.shape[0]
    shift = chunk.bit_length() - 1
    ri = lax.broadcasted_iota(jnp.int32, (rows, rows), 0)
    ci = lax.broadcasted_iota(jnp.int32, (rows, rows), 1)
    same = lax.shift_right_logical(ri, shift) == lax.shift_right_logical(ci, shift)
    low = same & (ri >= ci)
    strict = same & (ri > ci)
    beta_c = cols[:, hh:hh + 1]
    g_c = cols[:, 2 + hh:3 + hh]
    x_c = cols[:, 4 + hh:5 + hh]
    beta_r = grow[hh:hh + 1, :]
    g_r = grow[2 + hh:3 + hh, :]
    decay = jnp.exp(jnp.where(low, g_c - g_r, -jnp.inf))
    a = jnp.where(strict, -(beta_c * kk * decay), 0.0)
    inv = jnp.where(ri == ci, 1.0, a)
    p = a
    span = 2
    while span < chunk:
        ps = _split(p)
        p = _dot_split(ps, ps)
        inv = inv + _dot_split(_split(inv), _split(p))
        span *= 2
    kb = k.astype(BF16)
    u_mat = _dot((inv * beta_r).astype(BF16), v.astype(BF16))
    w_mat = _dot((inv * (beta_r * jnp.exp(g_r))).astype(BF16), kb)
    qk_m = (qk * decay).astype(BF16)
    q_dec = q * jnp.exp(g_c)
    k_dec = k * jnp.exp(x_c)
    return u_mat, w_mat, qk_m, q_dec, k_dec, g_c + x_c


def _gate_cols(g_ref):
    g = g_ref[...]
    rows = g.shape[1]
    padded = jnp.concatenate([g, jnp.zeros((HEAD_DIM - SUBLANES, rows), F32)], axis=0)
    return g, padded.T


def _gated_out(o, z, onorm):
    ms = jnp.mean(o * o, axis=-1, keepdims=True)
    return ((o * lax.rsqrt(ms + EPS)) * onorm) * (z * _sigmoid(z))


def _gdn_prompt_kernel(q_ref, k_ref, v_ref, z_ref, g_ref, on_ref, o_ref, s_ref, *, chunk):
    grp = pl.program_id(2)

    @pl.when(grp == 0)
    def _():
        s_ref[...] = jnp.zeros(s_ref.shape, F32)

    q = q_ref[...]
    k = k_ref[...]
    rows = q.shape[0]
    qb = q.astype(BF16)
    kb = k.astype(BF16)
    kk = _dot_nt(kb, kb)
    qk = _dot_nt(qb, kb)
    grow, cols = _gate_cols(g_ref)
    onorm = on_ref[...]
    for hh in range(2):
        hcols = slice(hh * HEAD_DIM, (hh + 1) * HEAD_DIM)
        u_mat, w_mat, qk_m, q_dec, k_dec, gtot = _gdn_chunk_parts(
            q, k, v_ref[:, hcols], kk, qk, cols, grow, hh, chunk)
        wb = w_mat.astype(BF16)
        qdb = q_dec.astype(BF16)
        kdb = k_dec.astype(BF16)
        state = s_ref[hh]
        v_new, q_state = [], []
        for n in range(rows // chunk):
            rs = slice(n * chunk, (n + 1) * chunk)
            res = _dot(jnp.concatenate([wb[rs], qdb[rs]], axis=0), state.astype(BF16))
            v_n = u_mat[rs] - res[0:chunk]
            v_new.append(v_n)
            q_state.append(res[chunk:2 * chunk])
            g_last = jnp.exp(gtot[n * chunk:n * chunk + 1, :])
            state = g_last * state + _dot_tn(kdb[rs], v_n.astype(BF16))
        s_ref[hh] = state
        o = jnp.concatenate(q_state, axis=0) + _dot(qk_m, jnp.concatenate(v_new, axis=0).astype(BF16))
        o_ref[:, hcols] = _gated_out(o, z_ref[:, hcols], onorm).astype(BF16)


def _gdn_prompt(qkv, qkvz, gates, o_norm, *, batch, seq_len):
    t = qkv.shape[0]
    r = GDN_ROWS
    groups = seq_len // r
    kern = functools.partial(_gdn_prompt_kernel, chunk=PROMPT_CHUNK)
    row = lambda b, h, g: b * groups + g
    return pl.pallas_call(
        kern,
        grid=(batch, QK_HEADS, groups),
        in_specs=[
            pl.BlockSpec((r, HEAD_DIM), lambda b, h, g: (row(b, h, g), h)),
            pl.BlockSpec((r, HEAD_DIM), lambda b, h, g: (row(b, h, g), QK_HEADS + h)),
            pl.BlockSpec((r, 2 * HEAD_DIM), lambda b, h, g: (row(b, h, g), QK_HEADS + h)),
            pl.BlockSpec((r, 2 * HEAD_DIM), lambda b, h, g: (row(b, h, g), 2 * QK_HEADS + h)),
            pl.BlockSpec((None, SUBLANES, r), lambda b, h, g: (h, 0, row(b, h, g))),
            pl.BlockSpec((1, HEAD_DIM), lambda b, h, g: (0, 0)),
        ],
        out_specs=[
            pl.BlockSpec((r, 2 * HEAD_DIM), lambda b, h, g: (row(b, h, g), h)),
            pl.BlockSpec((None, 2, HEAD_DIM, HEAD_DIM), lambda b, h, g: (b, h, 0, 0)),
        ],
        out_shape=[
            jax.ShapeDtypeStruct((t, VAL_DIM), BF16),
            jax.ShapeDtypeStruct((batch, V_HEADS, HEAD_DIM, HEAD_DIM), F32),
        ],
        compiler_params=_params(("parallel", "parallel", "arbitrary")),
        name="gdn_prompt",
    )(qkv, qkv, qkv, qkvz, gates, o_norm)


def _gdn_sample_kernel(q_ref, k_ref, v_ref, z_ref, g_ref, on_ref, s0_ref, o_ref, s_ref,
                       lhs_scr, u_scr, kd_scr, gl_scr, vn_scr, os_scr, *, chunk):
    q = q_ref[...]
    k = k_ref[...]
    rows = q.shape[0]
    nseq = rows // chunk
    qb = q.astype(BF16)
    kb = k.astype(BF16)
    kk = _dot_nt(kb, kb)
    qk = _dot_nt(qb, kb)
    grow, cols = _gate_cols(g_ref)
    onorm = on_ref[...]
    qk_masked = []
    for hh in range(2):
        hcols = slice(hh * HEAD_DIM, (hh + 1) * HEAD_DIM)
        u_mat, w_mat, qk_m, q_dec, k_dec, gtot = _gdn_chunk_parts(
            q, k, v_ref[:, hcols], kk, qk, cols, grow, hh, chunk)
        qk_masked.append(qk_m)
        lhs_scr[hh, :, 0:chunk, :] = w_mat.reshape(nseq, chunk, HEAD_DIM)
        lhs_scr[hh, :, chunk:2 * chunk, :] = q_dec.reshape(nseq, chunk, HEAD_DIM)
        u_scr[hh] = u_mat
        kd_scr[hh] = k_dec
        gl_scr[hh] = jnp.broadcast_to(jnp.exp(gtot), (rows, HEAD_DIM))

    def per_sequence(s, carry):
        r0 = pl.multiple_of(s * chunk, chunk)
        for hh in range(2):
            s0 = s0_ref[s, hh]
            res = _dot(lhs_scr[hh, s].astype(BF16), s0.astype(BF16))
            v_new = u_scr[hh, pl.ds(r0, chunk), :] - res[0:chunk]
            vn_scr[hh, pl.ds(r0, chunk), :] = v_new
            os_scr[hh, pl.ds(r0, chunk), :] = res[chunk:2 * chunk]
            g_last = gl_scr[hh, pl.ds(r0, chunk), :][0:1, :]
            upd = _dot_tn(kd_scr[hh, pl.ds(r0, chunk), :], v_new)
            s_ref[s, hh] = g_last * s0 + upd
        return carry

    lax.fori_loop(0, nseq, per_sequence, 0)
    for hh in range(2):
        hcols = slice(hh * HEAD_DIM, (hh + 1) * HEAD_DIM)
        o = os_scr[hh] + _dot(qk_masked[hh], vn_scr[hh].astype(BF16))
        o_ref[:, hcols] = _gated_out(o, z_ref[:, hcols], onorm).astype(BF16)


def _gdn_sample(qkv, qkvz, gates, o_norm, s0, *, chunk):
    t = qkv.shape[0]
    r = GDN_ROWS
    nseq = r // chunk
    kern = functools.partial(_gdn_sample_kernel, chunk=chunk)
    return pl.pallas_call(
        kern,
        grid=(t // r, QK_HEADS),
        in_specs=[
            pl.BlockSpec((r, HEAD_DIM), lambda i, h: (i, h)),
            pl.BlockSpec((r, HEAD_DIM), lambda i, h: (i, QK_HEADS + h)),
            pl.BlockSpec((r, 2 * HEAD_DIM), lambda i, h: (i, QK_HEADS + h)),
            pl.BlockSpec((r, 2 * HEAD_DIM), lambda i, h: (i, 2 * QK_HEADS + h)),
            pl.BlockSpec((None, SUBLANES, r), lambda i, h: (h, 0, i)),
            pl.BlockSpec((1, HEAD_DIM), lambda i, h: (0, 0)),
            pl.BlockSpec((nseq, 2, HEAD_DIM, HEAD_DIM), lambda i, h: (i, h, 0, 0)),
        ],
        out_specs=[
            pl.BlockSpec((r, 2 * HEAD_DIM), lambda i, h: (i, h)),
            pl.BlockSpec((nseq, 2, HEAD_DIM, HEAD_DIM), lambda i, h: (i, h, 0, 0)),
        ],
        out_shape=[
            jax.ShapeDtypeStruct((t, VAL_DIM), BF16),
            jax.ShapeDtypeStruct(s0.shape, F32),
        ],
        scratch_shapes=[
            pltpu.VMEM((2, nseq, 2 * chunk, HEAD_DIM), F32),
            pltpu.VMEM((2, r, HEAD_DIM), F32),
            pltpu.VMEM((2, r, HEAD_DIM), F32),
            pltpu.VMEM((2, r, HEAD_DIM), F32),
            pltpu.VMEM((2, r, HEAD_DIM), F32),
            pltpu.VMEM((2, r, HEAD_DIM), F32),
        ],
        compiler_params=_params(("parallel", "parallel")),
        name="gdn_sample",
    )(qkv, qkv, qkv, qkvz, gates, o_norm, s0)


def _front_pad_rows(state, rows):
    b, w, c = state.shape
    return jnp.pad(state, ((0, 0), (0, rows - w), (0, 0))).reshape(b * rows, c)


def _trunk(x, conv_a, conv_b, ssm_b, wts, *, grouped):
    batch, seq_len, _ = x.shape
    t = batch * seq_len
    x0 = x.reshape(t, D_MODEL)
    if grouped:
        state_a = _front_pad_rows(conv_a, SUBLANES)
        state_b = _front_pad_rows(conv_b, SUBLANES)
        chunk = seq_len
    else:
        state_a = jnp.zeros((SUBLANES, D_MODEL), F32)
        state_b = None
        chunk = PROMPT_CHUNK

    u, y = _mixer_a(x0, wts["norm_a"], wts["w_in_a"], wts["w_conv_a"], state_a,
                    seq_len=seq_len, grouped=grouped)
    new_conv_a = u.reshape(batch, seq_len, D_MODEL)[:, seq_len - (CONV_A_WIDTH - 1):]
    x1 = _matmul_residual(y, wts["w_out_a"], x0)
    x2 = _mlp(x1, wts["mlp_norm0"], wts["w_up0"], wts["w_down0"], wts["final_norm"], final_norm=False)

    qkvz, gates = _gdn_in(x2, wts["norm_b"], wts["w_qkvz"], wts["w_gates_t"], wts["a_log"],
                          wts["dt_bias"], chunk=chunk)
    new_conv_b = qkvz.reshape(batch, seq_len, QKVZ_DIM)[:, seq_len - (CONV_B_WIDTH - 1):, :CONV_DIM]
    qkv = _gdn_conv(qkvz, state_b, wts["w_conv_b"], seq_len=seq_len, grouped=grouped)
    if grouped:
        o, s_new = _gdn_sample(qkv, qkvz, gates, wts["o_norm"], ssm_b, chunk=chunk)
    else:
        o, s_new = _gdn_prompt(qkv, qkvz, gates, wts["o_norm"], batch=batch, seq_len=seq_len)
    x3 = _matmul_residual(o, wts["w_out_b"], x2)
    x4 = _mlp(x3, wts["mlp_norm1"], wts["w_up1"], wts["w_down1"], wts["final_norm"], final_norm=True)
    return (x4.reshape(batch, seq_len, D_MODEL), new_conv_a[None], new_conv_b[None], s_new[None])


def kernel(x_prompt, x_sample, state_conv_a, state_conv_b, state_ssm_b, norm_a, w_in_a, w_conv_a,
           w_out_a, norm_b, w_in_b, w_conv_b, a_log_b, dt_bias_b, o_norm_b, w_out_b, mlp_norm,
           w_up, w_down, final_norm):
    wts = {
        "norm_a": norm_a[0][None],
        "w_in_a": w_in_a[0].astype(BF16),
        "w_conv_a": w_conv_a[0],
        "w_out_a": w_out_a[0].astype(BF16),
        "norm_b": norm_b[0][None],
        "w_qkvz": w_in_b[0][:, :QKVZ_DIM].astype(BF16),
        "w_gates_t": w_in_b[0][:, QKVZ_DIM:].T.astype(BF16),
        "w_conv_b": w_conv_b[0],
        "a_log": a_log_b[0][:, None],
        "dt_bias": dt_bias_b[0][:, None],
        "o_norm": o_norm_b[0][None],
        "w_out_b": w_out_b[0].astype(BF16),
        "mlp_norm0": mlp_norm[0][None],
        "mlp_norm1": mlp_norm[1][None],
        "w_up0": w_up[0].astype(BF16),
        "w_up1": w_up[1].astype(BF16),
        "w_down0": w_down[0].astype(BF16),
        "w_down1": w_down[1].astype(BF16),
        "final_norm": final_norm[None],
    }
    bp = x_prompt.shape[0]
    y_p, ca_p, cb_p, s_p = _trunk(x_prompt, None, None, None, wts, grouped=False)
    y_s, ca_s, cb_s, s_s = _trunk(x_sample, state_conv_a[0], state_conv_b[0], state_ssm_b[0], wts,
                                  grouped=True)
    del bp
    return (y_p, y_s, ca_p, cb_p, s_p, ca_s, cb_s, s_s)
```

```python
import functools

import jax
import jax.numpy as jnp
from jax import lax
from jax.experimental import pallas as pl
from jax.experimental.pallas import tpu as pltpu

D_MODEL = 2048
D_FF = 4 * D_MODEL
HEAD_DIM = 128
QK_HEADS = D_MODEL // HEAD_DIM
V_HEADS = 2 * QK_HEADS
KEY_DIM = QK_HEADS * HEAD_DIM
VAL_DIM = V_HEADS * HEAD_DIM
CONV_DIM = 2 * KEY_DIM + VAL_DIM
QKVZ_DIM = CONV_DIM + VAL_DIM
CONV_A_WIDTH = 3
CONV_B_WIDTH = 4
PROMPT_CHUNK = 64
EPS = 1e-6

F32 = jnp.float32
BF16 = jnp.bfloat16

SUBLANES = 8
ROW_TILE = 512
GDN_ROWS = 256
GDN_PAIRS = 2
SEQ_UNROLL = 8
VMEM_LIMIT = 56 * 1024 * 1024


def _params(semantics):
    return pltpu.CompilerParams(dimension_semantics=semantics, vmem_limit_bytes=VMEM_LIMIT)


def _rmsnorm(x, w):
    ms = jnp.mean(x * x, axis=-1, keepdims=True)
    return (x * lax.rsqrt(ms + EPS)) * w


def _dot(a, b):
    return jnp.dot(a, b, preferred_element_type=F32)


def _dot_nt(a, b):
    return lax.dot_general(a, b, (((1,), (1,)), ((), ())), preferred_element_type=F32)


def _dot_tn(a, b):
    return lax.dot_general(a, b, (((0,), (0,)), ((), ())), preferred_element_type=F32)


def _sigmoid(x):
    return 1.0 / (1.0 + jnp.exp(-x))


def _softplus(x):
    return jnp.maximum(x, 0.0) + jnp.log1p(jnp.exp(-jnp.abs(x)))


def _prev_rows_seq(u, halo, j):
    tm = u.shape[0]
    return pltpu.roll(jnp.concatenate([u, halo], axis=0), j, 0)[0:tm]


def _prev_rows_grouped(u, state, j, width):
    tm = u.shape[0]
    rmod = lax.broadcasted_iota(jnp.int32, u.shape, 0) & (SUBLANES - 1)
    back = width - 1 - j
    st = state if back == 0 else pltpu.roll(state, tm - back, 0)
    return jnp.where(rmod < j, st, pltpu.roll(u, j, 0))


def _mixer_a_kernel(x_ref, nw_ref, wb_ref, wc_ref, wx_ref, cw_ref, st_ref, u_ref, y_ref,
                    hs_ref, carry_ref, *, tiles_per_seq, grouped):
    i = pl.program_id(0)
    j = pl.program_id(1)

    @pl.when(j == 0)
    def _():
        hs_ref[...] = _rmsnorm(x_ref[...], nw_ref[...]).astype(BF16)

    hs = hs_ref[...]
    gate_b = _dot(hs, wb_ref[...])
    u = _dot(hs, wc_ref[...]) * _dot(hs, wx_ref[...])
    u_ref[...] = u
    tm = u.shape[0]
    if grouped:
        state = st_ref[...]
        prev1 = _prev_rows_grouped(u, state, 1, CONV_A_WIDTH)
        prev2 = _prev_rows_grouped(u, state, 2, CONV_A_WIDTH)
    else:
        halo = jnp.where(i % tiles_per_seq != 0, carry_ref[j], 0.0)
        prev1 = _prev_rows_seq(u, halo, 1)
        prev2 = _prev_rows_seq(u, halo, 2)
        carry_ref[j] = u[tm - SUBLANES:tm]
    y = prev2 * cw_ref[0:1, :] + prev1 * cw_ref[1:2, :] + u * cw_ref[2:3, :]
    y_ref[...] = (gate_b * y).astype(BF16)


def _mixer_a(x, norm_w, w_in, conv_w, state, *, seq_len, grouped):
    t = x.shape[0]
    tm, tn = ROW_TILE, 512
    nj = D_MODEL // tn
    if grouped:
        st_spec = pl.BlockSpec((tm, tn), lambda i, j: (i, j))
    else:
        st_spec = pl.BlockSpec((SUBLANES, tn), lambda i, j: (0, j))
    kern = functools.partial(_mixer_a_kernel, tiles_per_seq=max(seq_len // tm, 1), grouped=grouped)
    return pl.pallas_call(
        kern,
        grid=(t // tm, nj),
        in_specs=[
            pl.BlockSpec((tm, D_MODEL), lambda i, j: (i, 0)),
            pl.BlockSpec((1, D_MODEL), lambda i, j: (0, 0)),
            pl.BlockSpec((D_MODEL, tn), lambda i, j: (0, j)),
            pl.BlockSpec((D_MODEL, tn), lambda i, j: (0, nj + j)),
            pl.BlockSpec((D_MODEL, tn), lambda i, j: (0, 2 * nj + j)),
            pl.BlockSpec((CONV_A_WIDTH, tn), lambda i, j: (0, j)),
            st_spec,
        ],
        out_specs=[
            pl.BlockSpec((tm, tn), lambda i, j: (i, j)),
            pl.BlockSpec((tm, tn), lambda i, j: (i, j)),
        ],
        out_shape=[
            jax.ShapeDtypeStruct((t, D_MODEL), F32),
            jax.ShapeDtypeStruct((t, D_MODEL), BF16),
        ],
        scratch_shapes=[
            pltpu.VMEM((tm, D_MODEL), BF16),
            pltpu.VMEM((nj, SUBLANES, tn), F32),
        ],
        compiler_params=_params(("arbitrary", "arbitrary")),
        name="mixer_a",
    )(x, norm_w, w_in, w_in, w_in, conv_w, state)


def _matmul_residual_kernel(a_ref, w_ref, r_ref, o_ref):
    o_ref[...] = r_ref[...] + _dot(a_ref[...], w_ref[...])


def _matmul_residual(a, w, res):
    t, k = a.shape
    n = w.shape[1]
    tm, tn = ROW_TILE, 512
    return pl.pallas_call(
        _matmul_residual_kernel,
        grid=(t // tm, n // tn),
        in_specs=[
            pl.BlockSpec((tm, k), lambda i, j: (i, 0)),
            pl.BlockSpec((k, tn), lambda i, j: (0, j)),
            pl.BlockSpec((tm, tn), lambda i, j: (i, j)),
        ],
        out_specs=pl.BlockSpec((tm, tn), lambda i, j: (i, j)),
        out_shape=jax.ShapeDtypeStruct((t, n), F32),
        compiler_params=_params(("parallel", "arbitrary")),
        name="matmul_residual",
    )(a, w, res)


def _mlp_kernel(x_ref, nw_ref, wu_ref, wd_ref, fw_ref, o_ref, hs_ref, acc_ref, *, final_norm):
    j = pl.program_id(1)

    @pl.when(j == 0)
    def _():
        x = x_ref[...]
        hs_ref[...] = _rmsnorm(x, nw_ref[...]).astype(BF16)
        acc_ref[...] = x

    a = _dot(hs_ref[...], wu_ref[...])
    a = jnp.square(jnp.maximum(a, 0.0)).astype(BF16)
    acc_ref[...] += _dot(a, wd_ref[...])

    @pl.when(j == pl.num_programs(1) - 1)
    def _():
        r = acc_ref[...]
        o_ref[...] = _rmsnorm(r, fw_ref[...]) if final_norm else r


def _mlp(x, norm_w, w_up, w_down, final_w, *, final_norm):
    t = x.shape[0]
    tm, tf = ROW_TILE, 1024
    kern = functools.partial(_mlp_kernel, final_norm=final_norm)
    return pl.pallas_call(
        kern,
        grid=(t // tm, D_FF // tf),
        in_specs=[
            pl.BlockSpec((tm, D_MODEL), lambda i, j: (i, 0)),
            pl.BlockSpec((1, D_MODEL), lambda i, j: (0, 0)),
            pl.BlockSpec((D_MODEL, tf), lambda i, j: (0, j)),
            pl.BlockSpec((tf, D_MODEL), lambda i, j: (j, 0)),
            pl.BlockSpec((1, D_MODEL), lambda i, j: (0, 0)),
        ],
        out_specs=pl.BlockSpec((tm, D_MODEL), lambda i, j: (i, 0)),
        out_shape=jax.ShapeDtypeStruct((t, D_MODEL), F32),
        scratch_shapes=[
            pltpu.VMEM((tm, D_MODEL), BF16),
            pltpu.VMEM((tm, D_MODEL), F32),
        ],
        compiler_params=_params(("parallel", "arbitrary")),
        name="mlp",
    )(x, norm_w, w_up, w_down, final_w)


def _gdn_in_kernel(x_ref, nw_ref, w_ref, wg_ref, alog_ref, dtb_ref, qkvz_ref, gates_ref, hs_ref,
                   *, chunk):
    j = pl.program_id(1)

    @pl.when(j == 0)
    def _():
        hs = _rmsnorm(x_ref[...], nw_ref[...]).astype(BF16)
        hs_ref[...] = hs
        raw = _dot_nt(wg_ref[...], hs)
        tm = raw.shape[1]
        beta = _sigmoid(raw[0:V_HEADS])
        g = -jnp.exp(alog_ref[...]) * _softplus(raw[V_HEADS:2 * V_HEADS] + dtb_ref[...])
        pos = lax.broadcasted_iota(jnp.int32, g.shape, 1) & (chunk - 1)
        csum = g
        ssum = g
        s = 1
        while s < chunk:
            csum = csum + jnp.where(pos >= s, pltpu.roll(csum, s, 1), 0.0)
            ssum = ssum + jnp.where(pos < chunk - s, pltpu.roll(ssum, tm - s, 1), 0.0)
            s *= 2
        rest = ssum - g
        zero2 = jnp.zeros((2, tm), F32)
        for p in range(QK_HEADS):
            gates_ref[p, 0:2, :] = beta[2 * p:2 * p + 2]
            gates_ref[p, 2:4, :] = csum[2 * p:2 * p + 2]
            gates_ref[p, 4:6, :] = rest[2 * p:2 * p + 2]
            gates_ref[p, 6:8, :] = zero2

    qkvz_ref[...] = _dot(hs_ref[...], w_ref[...])


def _gdn_in(x, norm_w, w_qkvz, w_gates_t, a_log_col, dt_bias_col, *, chunk):
    t = x.shape[0]
    tm, tn = ROW_TILE, 1024
    kern = functools.partial(_gdn_in_kernel, chunk=chunk)
    return pl.pallas_call(
        kern,
        grid=(t // tm, QKVZ_DIM // tn),
        in_specs=[
            pl.BlockSpec((tm, D_MODEL), lambda i, j: (i, 0)),
            pl.BlockSpec((1, D_MODEL), lambda i, j: (0, 0)),
            pl.BlockSpec((D_MODEL, tn), lambda i, j: (0, j)),
            pl.BlockSpec((2 * V_HEADS, D_MODEL), lambda i, j: (0, 0)),
            pl.BlockSpec((V_HEADS, 1), lambda i, j: (0, 0)),
            pl.BlockSpec((V_HEADS, 1), lambda i, j: (0, 0)),
        ],
        out_specs=[
            pl.BlockSpec((tm, tn), lambda i, j: (i, j)),
            pl.BlockSpec((QK_HEADS, SUBLANES, tm), lambda i, j: (0, 0, i)),
        ],
        out_shape=[
            jax.ShapeDtypeStruct((t, QKVZ_DIM), F32),
            jax.ShapeDtypeStruct((QK_HEADS, SUBLANES, t), F32),
        ],
        scratch_shapes=[pltpu.VMEM((tm, D_MODEL), BF16)],
        compiler_params=_params(("parallel", "arbitrary")),
        name="gdn_in",
    )(x, norm_w, w_qkvz, w_gates_t, a_log_col, dt_bias_col)


def _gdn_conv_kernel(x_ref, st_ref, cw_ref, o_ref, *, tiles_per_seq, grouped, tn):
    i = pl.program_id(0)
    j = pl.program_id(1)
    u = x_ref[...]
    if grouped:
        state = st_ref[...]
        prev = [_prev_rows_grouped(u, state, s, CONV_B_WIDTH) for s in (1, 2, 3)]
    else:
        halo = jnp.where(i % tiles_per_seq != 0, st_ref[...], 0.0)
        prev = [_prev_rows_seq(u, halo, s) for s in (1, 2, 3)]
    y = prev[2] * cw_ref[0:1, :] + prev[1] * cw_ref[1:2, :] + prev[0] * cw_ref[2:3, :] + u * cw_ref[3:4, :]
    y = y * _sigmoid(y)
    n_qk = 2 * KEY_DIM // tn

    @pl.when(j >= n_qk)
    def _():
        o_ref[...] = y

    @pl.when(j < n_qk)
    def _():
        scale = jnp.where(j < KEY_DIM // tn, HEAD_DIM ** -0.5, 1.0).astype(F32)
        for hd in range(tn // HEAD_DIM):
            cols = slice(hd * HEAD_DIM, (hd + 1) * HEAD_DIM)
            yh = y[:, cols]
            ss = jnp.sum(yh * yh, axis=-1, keepdims=True)
            o_ref[:, cols] = (yh * lax.rsqrt(ss + EPS)) * scale


def _gdn_conv(qkvz, state, conv_w, *, seq_len, grouped):
    t = qkvz.shape[0]
    tm, tn = ROW_TILE, 1024
    if grouped:
        st_spec = pl.BlockSpec((tm, tn), lambda i, j: (i, j))
        st = state
    else:
        blocks_per_tile = tm // SUBLANES
        st_spec = pl.BlockSpec((SUBLANES, tn), lambda i, j: (jnp.maximum(i * blocks_per_tile - 1, 0), j))
        st = qkvz
    kern = functools.partial(_gdn_conv_kernel, tiles_per_seq=max(seq_len // tm, 1), grouped=grouped, tn=tn)
    return pl.pallas_call(
        kern,
        grid=(t // tm, CONV_DIM // tn),
        in_specs=[
            pl.BlockSpec((tm, tn), lambda i, j: (i, j)),
            st_spec,
            pl.BlockSpec((CONV_B_WIDTH, tn), lambda i, j: (0, j)),
        ],
        out_specs=pl.BlockSpec((tm, tn), lambda i, j: (i, j)),
        out_shape=jax.ShapeDtypeStruct((t, CONV_DIM), F32),
        compiler_params=_params(("parallel", "parallel")),
        name="gdn_conv",
    )(qkvz, st, conv_w)


def _block_masks(rows, chunk):
    levels = chunk.bit_length() - 1
    ri = lax.broadcasted_iota(jnp.int32, (rows, rows), 0)
    ci = lax.broadcasted_iota(jnp.int32, (rows, rows), 1)
    same = [None] + [lax.shift_right_logical(ri, lb) == lax.shift_right_logical(ci, lb)
                     for lb in range(1, levels + 1)]
    return {
        "diag": ri == ci,
        "pair": same[1],
        "join": [same[lb + 1] & ~same[lb] for lb in range(1, levels)],
        "low": same[levels] & (ri >= ci),
        "strict": same[levels] & (ri > ci),
    }


def _gate_cols(g):
    rows = g.shape[1]
    padded = jnp.concatenate([g, jnp.zeros((HEAD_DIM - SUBLANES, rows), F32)], axis=0)
    return padded.T


def _gdn_chunk_parts(q_ref, k_ref, v_ref, gates, masks):
    heads = []
    for pp, grow in enumerate(gates):
        qcols = slice(pp * HEAD_DIM, (pp + 1) * HEAD_DIM)
        q = q_ref[:, qcols]
        k = k_ref[:, qcols]
        kb = k.astype(BF16)
        kk = _dot_nt(kb, kb)
        qk = _dot_nt(q.astype(BF16), kb)
        cols = _gate_cols(grow)
        for hh in range(2):
            hv = 2 * pp + hh
            beta_c = cols[:, hh:hh + 1]
            g_c = cols[:, 2 + hh:3 + hh]
            x_c = cols[:, 4 + hh:5 + hh]
            g_r = grow[2 + hh:3 + hh, :]
            decay = jnp.exp(jnp.where(masks["low"], g_c - g_r, -jnp.inf))
            eg = jnp.exp(g_c)
            v = v_ref[:, hv * HEAD_DIM:(hv + 1) * HEAD_DIM]
            heads.append({
                "l_mat": jnp.where(masks["strict"], beta_c * kk * decay, 0.0),
                "rhs": jnp.concatenate([v * beta_c, k * (beta_c * eg)], axis=1).astype(BF16),
                "qk_m": (qk * decay).astype(BF16),
                "q_dec": q * eg,
                "k_dec": k * jnp.exp(x_c),
                "g_tot": g_c + x_c,
            })
    invs = [jnp.where(masks["diag"], 1.0, jnp.where(masks["pair"], -h["l_mat"], 0.0)) for h in heads]
    for join in masks["join"]:
        invbs = [inv.astype(BF16) for inv in invs]
        cross = [_dot(jnp.where(join, h["l_mat"], 0.0).astype(BF16), ib).astype(BF16)
                 for h, ib in zip(heads, invbs)]
        invs = [inv - _dot(ib, cr) for inv, ib, cr in zip(invs, invbs, cross)]
    for h, inv in zip(heads, invs):
        uw = _dot(inv.astype(BF16), h["rhs"])
        h["u_mat"] = uw[:, :HEAD_DIM]
        h["w_mat"] = uw[:, HEAD_DIM:]
    return heads


def _gated_out(o, z, onorm):
    ms = jnp.mean(o * o, axis=-1, keepdims=True)
    return ((o * lax.rsqrt(ms + EPS)) * onorm) * (z * _sigmoid(z))


def _gdn_prompt_kernel(q_ref, k_ref, v_ref, z_ref, g_ref, on_ref, o_ref, s_ref, *, chunk, pairs):
    grp = pl.program_id(2)

    @pl.when(grp == 0)
    def _():
        s_ref[...] = jnp.zeros(s_ref.shape, F32)

    rows = q_ref.shape[0]
    masks = _block_masks(rows, chunk)
    heads = _gdn_chunk_parts(q_ref, k_ref, v_ref, [g_ref[pp] for pp in range(pairs)], masks)
    nh = len(heads)
    wbs = [h["w_mat"].astype(BF16) for h in heads]
    qdbs = [h["q_dec"].astype(BF16) for h in heads]
    kdbs = [h["k_dec"].astype(BF16) for h in heads]
    states = [s_ref[hv] for hv in range(nh)]
    v_new = [[] for _ in range(nh)]
    q_state = [[] for _ in range(nh)]
    for n in range(rows // chunk):
        rs = slice(n * chunk, (n + 1) * chunk)
        res = [_dot(jnp.concatenate([wb[rs], qdb[rs]], axis=0), st.astype(BF16))
               for wb, qdb, st in zip(wbs, qdbs, states)]
        for hv, h in enumerate(heads):
            v_n = h["u_mat"][rs] - res[hv][0:chunk]
            v_new[hv].append(v_n)
            q_state[hv].append(res[hv][chunk:2 * chunk])
            g_last = jnp.exp(h["g_tot"][n * chunk:n * chunk + 1, :])
            states[hv] = g_last * states[hv] + _dot_tn(kdbs[hv][rs], v_n.astype(BF16))
    onorm = on_ref[...]
    for hv, h in enumerate(heads):
        hcols = slice(hv * HEAD_DIM, (hv + 1) * HEAD_DIM)
        s_ref[hv] = states[hv]
        o = (jnp.concatenate(q_state[hv], axis=0)
             + _dot(h["qk_m"], jnp.concatenate(v_new[hv], axis=0).astype(BF16)))
        o_ref[:, hcols] = _gated_out(o, z_ref[:, hcols], onorm).astype(BF16)


def _gdn_prompt(qkv, qkvz, gates, o_norm, *, batch, seq_len):
    t = qkv.shape[0]
    r = GDN_ROWS
    groups = seq_len // r
    pairs = GDN_PAIRS
    hblocks = QK_HEADS // pairs
    qw, vw = pairs * HEAD_DIM, 2 * pairs * HEAD_DIM
    kern = functools.partial(_gdn_prompt_kernel, chunk=PROMPT_CHUNK, pairs=pairs)
    row = lambda b, h, g: b * groups + g
    return pl.pallas_call(
        kern,
        grid=(batch, hblocks, groups),
        in_specs=[
            pl.BlockSpec((r, qw), lambda b, h, g: (row(b, h, g), h)),
            pl.BlockSpec((r, qw), lambda b, h, g: (row(b, h, g), hblocks + h)),
            pl.BlockSpec((r, vw), lambda b, h, g: (row(b, h, g), hblocks + h)),
            pl.BlockSpec((r, vw), lambda b, h, g: (row(b, h, g), 2 * hblocks + h)),
            pl.BlockSpec((pairs, SUBLANES, r), lambda b, h, g: (h, 0, row(b, h, g))),
            pl.BlockSpec((1, HEAD_DIM), lambda b, h, g: (0, 0)),
        ],
        out_specs=[
            pl.BlockSpec((r, vw), lambda b, h, g: (row(b, h, g), h)),
            pl.BlockSpec((None, 2 * pairs, HEAD_DIM, HEAD_DIM), lambda b, h, g: (b, h, 0, 0)),
        ],
        out_shape=[
            jax.ShapeDtypeStruct((t, VAL_DIM), BF16),
            jax.ShapeDtypeStruct((batch, V_HEADS, HEAD_DIM, HEAD_DIM), F32),
        ],
        compiler_params=_params(("parallel", "parallel", "arbitrary")),
        name="gdn_prompt",
    )(qkv, qkv, qkv, qkvz, gates, o_norm)


def _gdn_sample_kernel(q_ref, k_ref, v_ref, z_ref, g_ref, on_ref, s0_ref, o_ref, s_ref,
                       lhs_scr, u_scr, kd_scr, gl_scr, vn_scr, os_scr, *, chunk):
    rows = q_ref.shape[0]
    nseq = rows // chunk
    masks = _block_masks(rows, chunk)
    heads = _gdn_chunk_parts(q_ref, k_ref, v_ref, [g_ref[...]], masks)
    for hh, h in enumerate(heads):
        lhs_scr[hh, :, 0:chunk, :] = h["w_mat"].reshape(nseq, chunk, HEAD_DIM)
        lhs_scr[hh, :, chunk:2 * chunk, :] = h["q_dec"].reshape(nseq, chunk, HEAD_DIM)
        u_scr[hh] = h["u_mat"]
        kd_scr[hh] = h["k_dec"]
        gl_scr[hh] = jnp.broadcast_to(jnp.exp(h["g_tot"]), (rows, HEAD_DIM))
    qk_masked = [h["qk_m"] for h in heads]
    onorm = on_ref[...]

    def per_sequences(step, carry):
        work = [(step * SEQ_UNROLL + lane, hh) for lane in range(SEQ_UNROLL) for hh in range(2)]
        rws = [pl.ds(pl.multiple_of(s * chunk, chunk), chunk) for s, _ in work]
        s0s = [s0_ref[s, hh] for s, hh in work]
        res = [_dot(lhs_scr[hh, s].astype(BF16), s0.astype(BF16))
               for (s, hh), s0 in zip(work, s0s)]
        v_new = [u_scr[hh, rw, :] - r[0:chunk] for (_, hh), rw, r in zip(work, rws, res)]
        upd = [_dot_tn(kd_scr[hh, rw, :], vn) for (_, hh), rw, vn in zip(work, rws, v_new)]
        for (s, hh), rw, s0, r, vn, up in zip(work, rws, s0s, res, v_new, upd):
            vn_scr[hh, rw, :] = vn
            os_scr[hh, rw, :] = r[chunk:2 * chunk]
            s_ref[s, hh] = gl_scr[hh, rw, :][0:1, :] * s0 + up
        return carry

    lax.fori_loop(0, nseq // SEQ_UNROLL, per_sequences, 0)
    for hh in range(2):
        hcols = slice(hh * HEAD_DIM, (hh + 1) * HEAD_DIM)
        o = os_scr[hh] + _dot(qk_masked[hh], vn_scr[hh].astype(BF16))
        o_ref[:, hcols] = _gated_out(o, z_ref[:, hcols], onorm).astype(BF16)


def _gdn_sample(qkv, qkvz, gates, o_norm, s0, *, chunk):
    t = qkv.shape[0]
    r = GDN_ROWS
    nseq = r // chunk
    kern = functools.partial(_gdn_sample_kernel, chunk=chunk)
    return pl.pallas_call(
        kern,
        grid=(t // r, QK_HEADS),
        in_specs=[
            pl.BlockSpec((r, HEAD_DIM), lambda i, h: (i, h)),
            pl.BlockSpec((r, HEAD_DIM), lambda i, h: (i, QK_HEADS + h)),
            pl.BlockSpec((r, 2 * HEAD_DIM), lambda i, h: (i, QK_HEADS + h)),
            pl.BlockSpec((r, 2 * HEAD_DIM), lambda i, h: (i, 2 * QK_HEADS + h)),
            pl.BlockSpec((None, SUBLANES, r), lambda i, h: (h, 0, i)),
            pl.BlockSpec((1, HEAD_DIM), lambda i, h: (0, 0)),
            pl.BlockSpec((nseq, 2, HEAD_DIM, HEAD_DIM), lambda i, h: (i, h, 0, 0)),
        ],
        out_specs=[
            pl.BlockSpec((r, 2 * HEAD_DIM), lambda i, h: (i, h)),
            pl.BlockSpec((nseq, 2, HEAD_DIM, HEAD_DIM), lambda i, h: (i, h, 0, 0)),
        ],
        out_shape=[
            jax.ShapeDtypeStruct((t, VAL_DIM), BF16),
            jax.ShapeDtypeStruct(s0.shape, F32),
        ],
        scratch_shapes=[
            pltpu.VMEM((2, nseq, 2 * chunk, HEAD_DIM), F32),
            pltpu.VMEM((2, r, HEAD_DIM), F32),
            pltpu.VMEM((2, r, HEAD_DIM), F32),
            pltpu.VMEM((2, r, HEAD_DIM), F32),
            pltpu.VMEM((2, r, HEAD_DIM), F32),
            pltpu.VMEM((2, r, HEAD_DIM), F32),
        ],
        compiler_params=_params(("parallel", "parallel")),
        name="gdn_sample",
    )(qkv, qkv, qkv, qkvz, gates, o_norm, s0)


def _front_pad_rows(state, rows):
    b, w, c = state.shape
    return jnp.pad(state, ((0, 0), (0, rows - w), (0, 0))).reshape(b * rows, c)


def _trunk(x, conv_a, conv_b, ssm_b, wts, *, grouped):
    batch, seq_len, _ = x.shape
    t = batch * seq_len
    x0 = x.reshape(t, D_MODEL)
    if grouped:
        state_a = _front_pad_rows(conv_a, SUBLANES)
        state_b = _front_pad_rows(conv_b, SUBLANES)
        chunk = seq_len
    else:
        state_a = jnp.zeros((SUBLANES, D_MODEL), F32)
        state_b = None
        chunk = PROMPT_CHUNK

    u, y = _mixer_a(x0, wts["norm_a"], wts["w_in_a"], wts["w_conv_a"], state_a,
                    seq_len=seq_len, grouped=grouped)
    new_conv_a = u.reshape(batch, seq_len, D_MODEL)[:, seq_len - (CONV_A_WIDTH - 1):]
    x1 = _matmul_residual(y, wts["w_out_a"], x0)
    x2 = _mlp(x1, wts["mlp_norm0"], wts["w_up0"], wts["w_down0"], wts["final_norm"], final_norm=False)

    qkvz, gates = _gdn_in(x2, wts["norm_b"], wts["w_qkvz"], wts["w_gates_t"], wts["a_log"],
                          wts["dt_bias"], chunk=chunk)
    new_conv_b = qkvz.reshape(batch, seq_len, QKVZ_DIM)[:, seq_len - (CONV_B_WIDTH - 1):, :CONV_DIM]
    qkv = _gdn_conv(qkvz, state_b, wts["w_conv_b"], seq_len=seq_len, grouped=grouped)
    if grouped:
        o, s_new = _gdn_sample(qkv, qkvz, gates, wts["o_norm"], ssm_b, chunk=chunk)
    else:
        o, s_new = _gdn_prompt(qkv, qkvz, gates, wts["o_norm"], batch=batch, seq_len=seq_len)
    x3 = _matmul_residual(o, wts["w_out_b"], x2)
    x4 = _mlp(x3, wts["mlp_norm1"], wts["w_up1"], wts["w_down1"], wts["final_norm"], final_norm=True)
    return (x4.reshape(batch, seq_len, D_MODEL), new_conv_a[None], new_conv_b[None], s_new[None])


def kernel(x_prompt, x_sample, state_conv_a, state_conv_b, state_ssm_b, norm_a, w_in_a, w_conv_a,
           w_out_a, norm_b, w_in_b, w_conv_b, a_log_b, dt_bias_b, o_norm_b, w_out_b, mlp_norm,
           w_up, w_down, final_norm):
    wts = {
        "norm_a": norm_a[0][None],
        "w_in_a": w_in_a[0].astype(BF16),
        "w_conv_a": w_conv_a[0],
        "w_out_a": w_out_a[0].astype(BF16),
        "norm_b": norm_b[0][None],
        "w_qkvz": w_in_b[0][:, :QKVZ_DIM].astype(BF16),
        "w_gates_t": w_in_b[0][:, QKVZ_DIM:].T.astype(BF16),
        "w_conv_b": w_conv_b[0],
        "a_log": a_log_b[0][:, None],
        "dt_bias": dt_bias_b[0][:, None],
        "o_norm": o_norm_b[0][None],
        "w_out_b": w_out_b[0].astype(BF16),
        "mlp_norm0": mlp_norm[0][None],
        "mlp_norm1": mlp_norm[1][None],
        "w_up0": w_up[0].astype(BF16),
        "w_up1": w_up[1].astype(BF16),
        "w_down0": w_down[0].astype(BF16),
        "w_down1": w_down[1].astype(BF16),
        "final_norm": final_norm[None],
    }
    bp = x_prompt.shape[0]
    y_p, ca_p, cb_p, s_p = _trunk(x_prompt, None, None, None, wts, grouped=False)
    y_s, ca_s, cb_s, s_s = _trunk(x_sample, state_conv_a[0], state_conv_b[0], state_ssm_b[0], wts,
                                  grouped=True)
    del bp
    return (y_p, y_s, ca_p, cb_p, s_p, ca_s, cb_s, s_s)
```

```python
import functools

import jax
import jax.numpy as jnp
from jax import lax
from jax.experimental import pallas as pl
from jax.experimental.pallas import tpu as pltpu

D_MODEL = 2048
D_FF = 4 * D_MODEL
HEAD_DIM = 128
QK_HEADS = D_MODEL // HEAD_DIM
V_HEADS = 2 * QK_HEADS
KEY_DIM = QK_HEADS * HEAD_DIM
VAL_DIM = V_HEADS * HEAD_DIM
CONV_DIM = 2 * KEY_DIM + VAL_DIM
QKVZ_DIM = CONV_DIM + VAL_DIM
CONV_A_WIDTH = 3
CONV_B_WIDTH = 4
PROMPT_CHUNK = 64
EPS = 1e-6

F32 = jnp.float32
BF16 = jnp.bfloat16

SUBLANES = 8
ROW_TILE = 1024
CONV_TILE = 512
CONV_STRIP = 32
GDN_ROWS = 256
GDN_PAIRS = 4
SEQ_UNROLL = 8
VMEM_LIMIT = 56 * 1024 * 1024


def _params(semantics):
    return pltpu.CompilerParams(dimension_semantics=semantics, vmem_limit_bytes=VMEM_LIMIT)


def _rmsnorm(x, w):
    ms = jnp.mean(x * x, axis=-1, keepdims=True)
    return (x * lax.rsqrt(ms + EPS)) * w


def _dot(a, b):
    return jnp.dot(a, b, preferred_element_type=F32)


def _dot_nt(a, b):
    return lax.dot_general(a, b, (((1,), (1,)), ((), ())), preferred_element_type=F32)


def _dot_tn(a, b):
    return lax.dot_general(a, b, (((0,), (0,)), ((), ())), preferred_element_type=F32)


def _sigmoid(x):
    return 1.0 / (1.0 + jnp.exp(-x))


def _softplus(x):
    return jnp.maximum(x, 0.0) + jnp.log1p(jnp.exp(-jnp.abs(x)))


def _prev_rows_seq(u, halo, j):
    tm = u.shape[0]
    return pltpu.roll(jnp.concatenate([u, halo], axis=0), j, 0)[0:tm]


def _prev_rows_grouped(u, state, j, width):
    tm = u.shape[0]
    rmod = lax.broadcasted_iota(jnp.int32, u.shape, 0) & (SUBLANES - 1)
    back = width - 1 - j
    st = state if back == 0 else pltpu.roll(state, tm - back, 0)
    return jnp.where(rmod < j, st, pltpu.roll(u, j, 0))


def _mixer_a_kernel(x_ref, nw_ref, wb_ref, wc_ref, wx_ref, cw_ref, st_ref, u_ref, y_ref,
                    hs_ref, carry_ref, *, tiles_per_seq, grouped):
    i = pl.program_id(0)
    j = pl.program_id(1)

    @pl.when(j == 0)
    def _():
        hs_ref[...] = _rmsnorm(x_ref[...], nw_ref[...]).astype(BF16)

    hs = hs_ref[...]
    gate_b = _dot(hs, wb_ref[...])
    u = _dot(hs, wc_ref[...]) * _dot(hs, wx_ref[...])
    u_ref[...] = u
    tm = u.shape[0]
    if grouped:
        state = st_ref[...]
        prev1 = _prev_rows_grouped(u, state, 1, CONV_A_WIDTH)
        prev2 = _prev_rows_grouped(u, state, 2, CONV_A_WIDTH)
    else:
        halo = jnp.where(i % tiles_per_seq != 0, carry_ref[j], 0.0)
        prev1 = _prev_rows_seq(u, halo, 1)
        prev2 = _prev_rows_seq(u, halo, 2)
        carry_ref[j] = u[tm - SUBLANES:tm]
    y = prev2 * cw_ref[0:1, :] + prev1 * cw_ref[1:2, :] + u * cw_ref[2:3, :]
    y_ref[...] = (gate_b * y).astype(BF16)


def _mixer_a(x, norm_w, w_in, conv_w, state, *, seq_len, grouped):
    t = x.shape[0]
    tm, tn = ROW_TILE, 512
    nj = D_MODEL // tn
    if grouped:
        st_spec = pl.BlockSpec((tm, tn), lambda i, j: (i, j))
    else:
        st_spec = pl.BlockSpec((SUBLANES, tn), lambda i, j: (0, j))
    kern = functools.partial(_mixer_a_kernel, tiles_per_seq=max(seq_len // tm, 1), grouped=grouped)
    return pl.pallas_call(
        kern,
        grid=(t // tm, nj),
        in_specs=[
            pl.BlockSpec((tm, D_MODEL), lambda i, j: (i, 0)),
            pl.BlockSpec((1, D_MODEL), lambda i, j: (0, 0)),
            pl.BlockSpec((D_MODEL, tn), lambda i, j: (0, j)),
            pl.BlockSpec((D_MODEL, tn), lambda i, j: (0, nj + j)),
            pl.BlockSpec((D_MODEL, tn), lambda i, j: (0, 2 * nj + j)),
            pl.BlockSpec((CONV_A_WIDTH, tn), lambda i, j: (0, j)),
            st_spec,
        ],
        out_specs=[
            pl.BlockSpec((tm, tn), lambda i, j: (i, j)),
            pl.BlockSpec((tm, tn), lambda i, j: (i, j)),
        ],
        out_shape=[
            jax.ShapeDtypeStruct((t, D_MODEL), F32),
            jax.ShapeDtypeStruct((t, D_MODEL), BF16),
        ],
        scratch_shapes=[
            pltpu.VMEM((tm, D_MODEL), BF16),
            pltpu.VMEM((nj, SUBLANES, tn), F32),
        ],
        compiler_params=_params(("arbitrary", "arbitrary")),
        name="mixer_a",
    )(x, norm_w, w_in, w_in, w_in, conv_w, state)


def _matmul_residual_kernel(a_ref, w_ref, r_ref, o_ref):
    o_ref[...] = r_ref[...] + _dot(a_ref[...], w_ref[...])


def _matmul_residual(a, w, res):
    t, k = a.shape
    n = w.shape[1]
    tm, tn = ROW_TILE, 512
    return pl.pallas_call(
        _matmul_residual_kernel,
        grid=(t // tm, n // tn),
        in_specs=[
            pl.BlockSpec((tm, k), lambda i, j: (i, 0)),
            pl.BlockSpec((k, tn), lambda i, j: (0, j)),
            pl.BlockSpec((tm, tn), lambda i, j: (i, j)),
        ],
        out_specs=pl.BlockSpec((tm, tn), lambda i, j: (i, j)),
        out_shape=jax.ShapeDtypeStruct((t, n), F32),
        compiler_params=_params(("parallel", "arbitrary")),
        name="matmul_residual",
    )(a, w, res)


def _mlp_kernel(x_ref, nw_ref, wu_ref, wd_ref, fw_ref, o_ref, hs_ref, *, final_norm):
    j = pl.program_id(1)

    @pl.when(j == 0)
    def _():
        x = x_ref[...]
        hs_ref[...] = _rmsnorm(x, nw_ref[...]).astype(BF16)
        o_ref[...] = x

    a = _dot(hs_ref[...], wu_ref[...])
    a = jnp.square(jnp.maximum(a, 0.0)).astype(BF16)
    o_ref[...] += _dot(a, wd_ref[...])

    if final_norm:
        @pl.when(j == pl.num_programs(1) - 1)
        def _():
            o_ref[...] = _rmsnorm(o_ref[...], fw_ref[...])


def _mlp(x, norm_w, w_up, w_down, final_w, *, layer, final_norm):
    t = x.shape[0]
    tm, tf = ROW_TILE, 512
    kern = functools.partial(_mlp_kernel, final_norm=final_norm)
    return pl.pallas_call(
        kern,
        grid=(t // tm, D_FF // tf),
        in_specs=[
            pl.BlockSpec((tm, D_MODEL), lambda i, j: (i, 0)),
            pl.BlockSpec((None, 1, D_MODEL), lambda i, j: (layer, 0, 0)),
            pl.BlockSpec((None, D_MODEL, tf), lambda i, j: (layer, 0, j)),
            pl.BlockSpec((None, tf, D_MODEL), lambda i, j: (layer, j, 0)),
            pl.BlockSpec((1, D_MODEL), lambda i, j: (0, 0)),
        ],
        out_specs=pl.BlockSpec((tm, D_MODEL), lambda i, j: (i, 0)),
        out_shape=jax.ShapeDtypeStruct((t, D_MODEL), F32),
        scratch_shapes=[pltpu.VMEM((tm, D_MODEL), BF16)],
        compiler_params=_params(("parallel", "arbitrary")),
        name="mlp",
    )(x, norm_w, w_up, w_down, final_w)


def _gdn_in_kernel(x_ref, nw_ref, w_ref, wg_ref, alog_ref, dtb_ref, qkvz_ref, gates_ref, hs_ref,
                   *, chunk):
    j = pl.program_id(1)

    @pl.when(j == 0)
    def _():
        hs = _rmsnorm(x_ref[...], nw_ref[...]).astype(BF16)
        hs_ref[...] = hs
        raw = _dot_nt(wg_ref[...], hs)
        tm = raw.shape[1]
        beta = _sigmoid(raw[0:V_HEADS])
        g = -jnp.exp(alog_ref[...]) * _softplus(raw[V_HEADS:2 * V_HEADS] + dtb_ref[...])
        pos = lax.broadcasted_iota(jnp.int32, g.shape, 1) & (chunk - 1)
        csum = g
        ssum = g
        s = 1
        while s < chunk:
            csum = csum + jnp.where(pos >= s, pltpu.roll(csum, s, 1), 0.0)
            ssum = ssum + jnp.where(pos < chunk - s, pltpu.roll(ssum, tm - s, 1), 0.0)
            s *= 2
        rest = ssum - g
        zero2 = jnp.zeros((2, tm), F32)
        for p in range(QK_HEADS):
            gates_ref[p, 0:2, :] = beta[2 * p:2 * p + 2]
            gates_ref[p, 2:4, :] = csum[2 * p:2 * p + 2]
            gates_ref[p, 4:6, :] = rest[2 * p:2 * p + 2]
            gates_ref[p, 6:8, :] = zero2

    qkvz_ref[...] = _dot_nt(hs_ref[...], w_ref[...])


def _gdn_in(x, norm_w, w_in_t, a_log_col, dt_bias_col, *, chunk):
    t = x.shape[0]
    tm, tn = ROW_TILE, 1024
    kern = functools.partial(_gdn_in_kernel, chunk=chunk)
    return pl.pallas_call(
        kern,
        grid=(t // tm, QKVZ_DIM // tn),
        in_specs=[
            pl.BlockSpec((tm, D_MODEL), lambda i, j: (i, 0)),
            pl.BlockSpec((1, D_MODEL), lambda i, j: (0, 0)),
            pl.BlockSpec((tn, D_MODEL), lambda i, j: (j, 0)),
            pl.BlockSpec((2 * V_HEADS, D_MODEL), lambda i, j: (QKVZ_DIM // (2 * V_HEADS), 0)),
            pl.BlockSpec((V_HEADS, 1), lambda i, j: (0, 0)),
            pl.BlockSpec((V_HEADS, 1), lambda i, j: (0, 0)),
        ],
        out_specs=[
            pl.BlockSpec((tm, tn), lambda i, j: (i, j)),
            pl.BlockSpec((QK_HEADS, SUBLANES, tm), lambda i, j: (0, 0, i)),
        ],
        out_shape=[
            jax.ShapeDtypeStruct((t, QKVZ_DIM), F32),
            jax.ShapeDtypeStruct((QK_HEADS, SUBLANES, t), F32),
        ],
        scratch_shapes=[pltpu.VMEM((tm, D_MODEL), BF16)],
        compiler_params=_params(("parallel", "arbitrary")),
        name="gdn_in",
    )(x, norm_w, w_in_t, w_in_t, a_log_col, dt_bias_col)


def _gdn_conv_kernel(x_ref, st_ref, cw_ref, o_ref, *, tiles_per_seq, grouped, tn):
    i = pl.program_id(0)
    j = pl.program_id(1)
    tm = x_ref.shape[0]
    n_qk = 2 * KEY_DIM // tn

    def run(normalise):
        scale = jnp.where(j < KEY_DIM // tn, HEAD_DIM ** -0.5, 1.0).astype(F32)
        for hd in range(tn // HEAD_DIM):
            cols = slice(hd * HEAD_DIM, (hd + 1) * HEAD_DIM)
            taps = [cw_ref[s:s + 1, cols] for s in range(CONV_B_WIDTH)]

            def finish(u, prev, rows):
                y = prev[2] * taps[0] + prev[1] * taps[1] + prev[0] * taps[2] + u * taps[3]
                o_ref[rows, cols] = y * _sigmoid(y)

            def strip(rb, carry):
                r0 = pl.multiple_of(rb * CONV_STRIP, CONV_STRIP)
                rows = pl.ds(r0, CONV_STRIP)
                u = x_ref[rows, cols]
                if grouped:
                    state = st_ref[rows, cols]
                    prev = [_prev_rows_grouped(u, state, s, CONV_B_WIDTH) for s in (1, 2, 3)]
                else:
                    halo = x_ref[pl.ds(pl.multiple_of(r0 - SUBLANES, SUBLANES), SUBLANES), cols]
                    prev = [_prev_rows_seq(u, halo, s) for s in (1, 2, 3)]
                finish(u, prev, rows)
                return carry

            first = 0
            if not grouped:
                u = x_ref[0:CONV_STRIP, cols]
                halo = jnp.where(i % tiles_per_seq != 0, st_ref[:, cols], 0.0)
                finish(u, [_prev_rows_seq(u, halo, s) for s in (1, 2, 3)], slice(0, CONV_STRIP))
                first = 1
            lax.fori_loop(first, tm // CONV_STRIP, strip, 0, unroll=4)
            if normalise:
                y = o_ref[:, cols]
                ss = jnp.sum(y * y, axis=-1, keepdims=True)
                o_ref[:, cols] = (y * lax.rsqrt(ss + EPS)) * scale

    @pl.when(j >= n_qk)
    def _():
        run(False)

    @pl.when(j < n_qk)
    def _():
        run(True)


def _gdn_conv(qkvz, state, conv_w, *, seq_len, grouped):
    t = qkvz.shape[0]
    tm, tn = CONV_TILE, 1024
    if grouped:
        st_spec = pl.BlockSpec((tm, tn), lambda i, j: (i, j))
        st = state
    else:
        blocks_per_tile = tm // SUBLANES
        st_spec = pl.BlockSpec((SUBLANES, tn), lambda i, j: (jnp.maximum(i * blocks_per_tile - 1, 0), j))
        st = qkvz
    kern = functools.partial(_gdn_conv_kernel, tiles_per_seq=max(seq_len // tm, 1), grouped=grouped, tn=tn)
    return pl.pallas_call(
        kern,
        grid=(t // tm, CONV_DIM // tn),
        in_specs=[
            pl.BlockSpec((tm, tn), lambda i, j: (i, j)),
            st_spec,
            pl.BlockSpec((CONV_B_WIDTH, tn), lambda i, j: (0, j)),
        ],
        out_specs=pl.BlockSpec((tm, tn), lambda i, j: (i, j)),
        out_shape=jax.ShapeDtypeStruct((t, CONV_DIM), F32),
        compiler_params=_params(("parallel", "parallel")),
        name="gdn_conv",
    )(qkvz, st, conv_w)


def _block_masks(rows, chunk):
    levels = chunk.bit_length() - 1
    ri = lax.broadcasted_iota(jnp.int32, (rows, rows), 0)
    ci = lax.broadcasted_iota(jnp.int32, (rows, rows), 1)
    same = [None] + [lax.shift_right_logical(ri, lb) == lax.shift_right_logical(ci, lb)
                     for lb in range(1, levels + 1)]
    return {
        "diag": ri == ci,
        "pair": same[1],
        "join": [same[lb + 1] & ~same[lb] for lb in range(1, levels)],
        "low": same[levels] & (ri >= ci),
        "strict": same[levels] & (ri > ci),
    }


def _gate_cols(g):
    rows = g.shape[1]
    padded = jnp.concatenate([g, jnp.zeros((HEAD_DIM - SUBLANES, rows), F32)], axis=0)
    return padded.T


def _gdn_chunk_parts(q_ref, k_ref, v_ref, gates, masks):
    heads = []
    for pp, grow in enumerate(gates):
        qcols = slice(pp * HEAD_DIM, (pp + 1) * HEAD_DIM)
        q = q_ref[:, qcols]
        k = k_ref[:, qcols]
        kb = k.astype(BF16)
        kk = _dot_nt(kb, kb)
        qk = _dot_nt(q.astype(BF16), kb)
        cols = _gate_cols(grow)
        for hh in range(2):
            hv = 2 * pp + hh
            beta_c = cols[:, hh:hh + 1]
            g_c = cols[:, 2 + hh:3 + hh]
            x_c = cols[:, 4 + hh:5 + hh]
            g_r = grow[2 + hh:3 + hh, :]
            decay = jnp.exp(jnp.where(masks["low"], g_c - g_r, -jnp.inf))
            eg = jnp.exp(g_c)
            v = v_ref[:, hv * HEAD_DIM:(hv + 1) * HEAD_DIM]
            heads.append({
                "l_mat": jnp.where(masks["strict"], beta_c * kk * decay, 0.0),
                "rhs": jnp.concatenate([v * beta_c, k * (beta_c * eg)], axis=1).astype(BF16),
                "qk_m": (qk * decay).astype(BF16),
                "q_dec": q * eg,
                "k_dec": k * jnp.exp(x_c),
                "g_tot": g_c + x_c,
            })
    invs = [jnp.where(masks["diag"], 1.0, jnp.where(masks["pair"], -h["l_mat"], 0.0)) for h in heads]
    for join in masks["join"]:
        invbs = [inv.astype(BF16) for inv in invs]
        cross = [_dot(jnp.where(join, h["l_mat"], 0.0).astype(BF16), ib).astype(BF16)
                 for h, ib in zip(heads, invbs)]
        invs = [inv - _dot(ib, cr) for inv, ib, cr in zip(invs, invbs, cross)]
    for h, inv in zip(heads, invs):
        uw = _dot(inv.astype(BF16), h["rhs"])
        h["u_mat"] = uw[:, :HEAD_DIM]
        h["w_mat"] = uw[:, HEAD_DIM:]
    return heads


def _gated_out(o, z, onorm):
    ms = jnp.mean(o * o, axis=-1, keepdims=True)
    return ((o * lax.rsqrt(ms + EPS)) * onorm) * (z * _sigmoid(z))


def _gdn_prompt_kernel(q_ref, k_ref, v_ref, z_ref, g_ref, on_ref, o_ref, s_ref, *, chunk, pairs):
    grp = pl.program_id(2)

    @pl.when(grp == 0)
    def _():
        s_ref[...] = jnp.zeros(s_ref.shape, F32)

    rows = q_ref.shape[0]
    masks = _block_masks(rows, chunk)
    heads = _gdn_chunk_parts(q_ref, k_ref, v_ref, [g_ref[pp] for pp in range(pairs)], masks)
    nh = len(heads)
    wbs = [h["w_mat"].astype(BF16) for h in heads]
    qdbs = [h["q_dec"].astype(BF16) for h in heads]
    kdbs = [h["k_dec"].astype(BF16) for h in heads]
    states = [s_ref[hv] for hv in range(nh)]
    v_new = [[] for _ in range(nh)]
    q_state = [[] for _ in range(nh)]
    for n in range(rows // chunk):
        rs = slice(n * chunk, (n + 1) * chunk)
        res = [_dot(jnp.concatenate([wb[rs], qdb[rs]], axis=0), st.astype(BF16))
               for wb, qdb, st in zip(wbs, qdbs, states)]
        for hv, h in enumerate(heads):
            v_n = h["u_mat"][rs] - res[hv][0:chunk]
            v_new[hv].append(v_n)
            q_state[hv].append(res[hv][chunk:2 * chunk])
            g_last = jnp.exp(h["g_tot"][n * chunk:n * chunk + 1, :])
            states[hv] = g_last * states[hv] + _dot_tn(kdbs[hv][rs], v_n.astype(BF16))
    onorm = on_ref[...]
    for hv, h in enumerate(heads):
        hcols = slice(hv * HEAD_DIM, (hv + 1) * HEAD_DIM)
        s_ref[hv] = states[hv]
        o = (jnp.concatenate(q_state[hv], axis=0)
             + _dot(h["qk_m"], jnp.concatenate(v_new[hv], axis=0).astype(BF16)))
        o_ref[:, hcols] = _gated_out(o, z_ref[:, hcols], onorm).astype(BF16)


def _gdn_prompt(qkv, qkvz, gates, o_norm, *, batch, seq_len):
    t = qkv.shape[0]
    r = GDN_ROWS
    groups = seq_len // r
    pairs = GDN_PAIRS
    hblocks = QK_HEADS // pairs
    qw, vw = pairs * HEAD_DIM, 2 * pairs * HEAD_DIM
    kern = functools.partial(_gdn_prompt_kernel, chunk=PROMPT_CHUNK, pairs=pairs)
    row = lambda b, h, g: b * groups + g
    return pl.pallas_call(
        kern,
        grid=(batch, hblocks, groups),
        in_specs=[
            pl.BlockSpec((r, qw), lambda b, h, g: (row(b, h, g), h)),
            pl.BlockSpec((r, qw), lambda b, h, g: (row(b, h, g), hblocks + h)),
            pl.BlockSpec((r, vw), lambda b, h, g: (row(b, h, g), hblocks + h)),
            pl.BlockSpec((r, vw), lambda b, h, g: (row(b, h, g), 2 * hblocks + h)),
            pl.BlockSpec((pairs, SUBLANES, r), lambda b, h, g: (h, 0, row(b, h, g))),
            pl.BlockSpec((1, HEAD_DIM), lambda b, h, g: (0, 0)),
        ],
        out_specs=[
            pl.BlockSpec((r, vw), lambda b, h, g: (row(b, h, g), h)),
            pl.BlockSpec((None, 2 * pairs, HEAD_DIM, HEAD_DIM), lambda b, h, g: (b, h, 0, 0)),
        ],
        out_shape=[
            jax.ShapeDtypeStruct((t, VAL_DIM), BF16),
            jax.ShapeDtypeStruct((batch, V_HEADS, HEAD_DIM, HEAD_DIM), F32),
        ],
        compiler_params=_params(("parallel", "parallel", "arbitrary")),
        name="gdn_prompt",
    )(qkv, qkv, qkv, qkvz, gates, o_norm)


def _gdn_sample_kernel(q_ref, k_ref, v_ref, z_ref, g_ref, on_ref, s0_ref, o_ref, s_ref,
                       lhs_scr, u_scr, kd_scr, gl_scr, vn_scr, os_scr, *, chunk):
    rows = q_ref.shape[0]
    nseq = rows // chunk
    masks = _block_masks(rows, chunk)
    heads = _gdn_chunk_parts(q_ref, k_ref, v_ref, [g_ref[...]], masks)
    for hh, h in enumerate(heads):
        lhs_scr[hh, :, 0:chunk, :] = h["w_mat"].reshape(nseq, chunk, HEAD_DIM)
        lhs_scr[hh, :, chunk:2 * chunk, :] = h["q_dec"].reshape(nseq, chunk, HEAD_DIM)
        u_scr[hh] = h["u_mat"]
        kd_scr[hh] = h["k_dec"]
        gl_scr[hh] = jnp.broadcast_to(jnp.exp(h["g_tot"]), (rows, HEAD_DIM))
    qk_masked = [h["qk_m"] for h in heads]
    onorm = on_ref[...]

    def per_sequences(step, carry):
        work = [(step * SEQ_UNROLL + lane, hh) for lane in range(SEQ_UNROLL) for hh in range(2)]
        rws = [pl.ds(pl.multiple_of(s * chunk, chunk), chunk) for s, _ in work]
        s0s = [s0_ref[s, hh] for s, hh in work]
        res = [_dot(lhs_scr[hh, s].astype(BF16), s0.astype(BF16))
               for (s, hh), s0 in zip(work, s0s)]
        v_new = [u_scr[hh, rw, :] - r[0:chunk] for (_, hh), rw, r in zip(work, rws, res)]
        upd = [_dot_tn(kd_scr[hh, rw, :], vn) for (_, hh), rw, vn in zip(work, rws, v_new)]
        for (s, hh), rw, s0, r, vn, up in zip(work, rws, s0s, res, v_new, upd):
            vn_scr[hh, rw, :] = vn
            os_scr[hh, rw, :] = r[chunk:2 * chunk]
            s_ref[s, hh] = gl_scr[hh, rw, :][0:1, :] * s0 + up
        return carry

    lax.fori_loop(0, nseq // SEQ_UNROLL, per_sequences, 0)
    for hh in range(2):
        hcols = slice(hh * HEAD_DIM, (hh + 1) * HEAD_DIM)
        o = os_scr[hh] + _dot(qk_masked[hh], vn_scr[hh].astype(BF16))
        o_ref[:, hcols] = _gated_out(o, z_ref[:, hcols], onorm).astype(BF16)


def _gdn_sample(qkv, qkvz, gates, o_norm, s0, *, chunk):
    t = qkv.shape[0]
    r = GDN_ROWS
    nseq = r // chunk
    kern = functools.partial(_gdn_sample_kernel, chunk=chunk)
    return pl.pallas_call(
        kern,
        grid=(t // r, QK_HEADS),
        in_specs=[
            pl.BlockSpec((r, HEAD_DIM), lambda i, h: (i, h)),
            pl.BlockSpec((r, HEAD_DIM), lambda i, h: (i, QK_HEADS + h)),
            pl.BlockSpec((r, 2 * HEAD_DIM), lambda i, h: (i, QK_HEADS + h)),
            pl.BlockSpec((r, 2 * HEAD_DIM), lambda i, h: (i, 2 * QK_HEADS + h)),
            pl.BlockSpec((None, SUBLANES, r), lambda i, h: (h, 0, i)),
            pl.BlockSpec((1, HEAD_DIM), lambda i, h: (0, 0)),
            pl.BlockSpec((nseq, 2, HEAD_DIM, HEAD_DIM), lambda i, h: (i, h, 0, 0)),
        ],
        out_specs=[
            pl.BlockSpec((r, 2 * HEAD_DIM), lambda i, h: (i, h)),
            pl.BlockSpec((nseq, 2, HEAD_DIM, HEAD_DIM), lambda i, h: (i, h, 0, 0)),
        ],
        out_shape=[
            jax.ShapeDtypeStruct((t, VAL_DIM), BF16),
            jax.ShapeDtypeStruct(s0.shape, F32),
        ],
        scratch_shapes=[
            pltpu.VMEM((2, nseq, 2 * chunk, HEAD_DIM), F32),
            pltpu.VMEM((2, r, HEAD_DIM), F32),
            pltpu.VMEM((2, r, HEAD_DIM), F32),
            pltpu.VMEM((2, r, HEAD_DIM), F32),
            pltpu.VMEM((2, r, HEAD_DIM), F32),
            pltpu.VMEM((2, r, HEAD_DIM), F32),
        ],
        compiler_params=_params(("parallel", "parallel")),
        name="gdn_sample",
    )(qkv, qkv, qkv, qkvz, gates, o_norm, s0)


def _front_pad_rows(state, rows):
    b, w, c = state.shape
    return jnp.pad(state, ((0, 0), (0, rows - w), (0, 0))).reshape(b * rows, c)


def _trunk(x, conv_a, conv_b, ssm_b, wts, *, grouped):
    batch, seq_len, _ = x.shape
    t = batch * seq_len
    x0 = x.reshape(t, D_MODEL)
    if grouped:
        state_a = _front_pad_rows(conv_a, SUBLANES)
        state_b = _front_pad_rows(conv_b, SUBLANES)
        chunk = seq_len
    else:
        state_a = jnp.zeros((SUBLANES, D_MODEL), F32)
        state_b = None
        chunk = PROMPT_CHUNK

    u, y = _mixer_a(x0, wts["norm_a"], wts["w_in_a"], wts["w_conv_a"], state_a,
                    seq_len=seq_len, grouped=grouped)
    new_conv_a = u.reshape(batch, seq_len, D_MODEL)[:, seq_len - (CONV_A_WIDTH - 1):]
    x1 = _matmul_residual(y, wts["w_out_a"], x0)
    x2 = _mlp(x1, wts["mlp_norm"], wts["w_up"], wts["w_down"], wts["final_norm"], layer=0,
              final_norm=False)

    qkvz, gates = _gdn_in(x2, wts["norm_b"], wts["w_in_b_t"], wts["a_log"], wts["dt_bias"],
                          chunk=chunk)
    new_conv_b = qkvz.reshape(batch, seq_len, QKVZ_DIM)[:, seq_len - (CONV_B_WIDTH - 1):, :CONV_DIM]
    qkv = _gdn_conv(qkvz, state_b, wts["w_conv_b"], seq_len=seq_len, grouped=grouped)
    if grouped:
        o, s_new = _gdn_sample(qkv, qkvz, gates, wts["o_norm"], ssm_b, chunk=chunk)
    else:
        o, s_new = _gdn_prompt(qkv, qkvz, gates, wts["o_norm"], batch=batch, seq_len=seq_len)
    x3 = _matmul_residual(o, wts["w_out_b"], x2)
    x4 = _mlp(x3, wts["mlp_norm"], wts["w_up"], wts["w_down"], wts["final_norm"], layer=1,
              final_norm=True)
    return (x4.reshape(batch, seq_len, D_MODEL), new_conv_a[None], new_conv_b[None], s_new[None])


def kernel(x_prompt, x_sample, state_conv_a, state_conv_b, state_ssm_b, norm_a, w_in_a, w_conv_a,
           w_out_a, norm_b, w_in_b, w_conv_b, a_log_b, dt_bias_b, o_norm_b, w_out_b, mlp_norm,
           w_up, w_down, final_norm):
    wts = {
        "norm_a": norm_a[0][None],
        "w_in_a": w_in_a[0].astype(BF16),
        "w_conv_a": w_conv_a[0],
        "w_out_a": w_out_a[0].astype(BF16),
        "norm_b": norm_b[0][None],
        "w_in_b_t": w_in_b[0].T.astype(BF16),
        "w_conv_b": w_conv_b[0],
        "a_log": a_log_b[0][:, None],
        "dt_bias": dt_bias_b[0][:, None],
        "o_norm": o_norm_b[0][None],
        "w_out_b": w_out_b[0].astype(BF16),
        "mlp_norm": mlp_norm[:, None, :],
        "w_up": w_up.astype(BF16),
        "w_down": w_down.astype(BF16),
        "final_norm": final_norm[None],
    }
    y_p, ca_p, cb_p, s_p = _trunk(x_prompt, None, None, None, wts, grouped=False)
    y_s, ca_s, cb_s, s_s = _trunk(x_sample, state_conv_a[0], state_conv_b[0], state_ssm_b[0], wts,
                                  grouped=True)
    return (y_p, y_s, ca_p, cb_p, s_p, ca_s, cb_s, s_s)
```

```python
import functools

import jax
import jax.numpy as jnp
from jax import lax
from jax.experimental import pallas as pl
from jax.experimental.pallas import tpu as pltpu

D_MODEL = 2048
D_FF = 4 * D_MODEL
HEAD_DIM = 128
QK_HEADS = D_MODEL // HEAD_DIM
V_HEADS = 2 * QK_HEADS
KEY_DIM = QK_HEADS * HEAD_DIM
VAL_DIM = V_HEADS * HEAD_DIM
CONV_DIM = 2 * KEY_DIM + VAL_DIM
QKVZ_DIM = CONV_DIM + VAL_DIM
CONV_A_WIDTH = 3
CONV_B_WIDTH = 4
PROMPT_CHUNK = 64
EPS = 1e-6

F32 = jnp.float32
BF16 = jnp.bfloat16

SUBLANES = 8
ROW_TILE = 1024
CONV_TILE = 512
CONV_STRIP = 32
GDN_ROWS = 256
GDN_PAIRS = 8
SEQ_UNROLL = 16
VMEM_LIMIT = 56 * 1024 * 1024


def _params(semantics):
    return pltpu.CompilerParams(dimension_semantics=semantics, vmem_limit_bytes=VMEM_LIMIT)


def _rmsnorm(x, w):
    ms = jnp.mean(x * x, axis=-1, keepdims=True)
    return (x * lax.rsqrt(ms + EPS)) * w


def _dot(a, b):
    return jnp.dot(a, b, preferred_element_type=F32)


def _dot_nt(a, b):
    return lax.dot_general(a, b, (((1,), (1,)), ((), ())), preferred_element_type=F32)


def _dot_tn(a, b):
    return lax.dot_general(a, b, (((0,), (0,)), ((), ())), preferred_element_type=F32)


def _sigmoid(x):
    return 1.0 / (1.0 + jnp.exp(-x))


def _softplus(x):
    return jnp.maximum(x, 0.0) + jnp.log1p(jnp.exp(-jnp.abs(x)))


def _prev_rows_seq(u, halo, j):
    tm = u.shape[0]
    return pltpu.roll(jnp.concatenate([u, halo], axis=0), j, 0)[0:tm]


def _prev_rows_grouped(u, state, j, width):
    tm = u.shape[0]
    rmod = lax.broadcasted_iota(jnp.int32, u.shape, 0) & (SUBLANES - 1)
    back = width - 1 - j
    st = state if back == 0 else pltpu.roll(state, tm - back, 0)
    return jnp.where(rmod < j, st, pltpu.roll(u, j, 0))


def _mixer_a_kernel(x_ref, nw_ref, wb_ref, wc_ref, wx_ref, cw_ref, st_ref, u_ref, y_ref,
                    hs_ref, carry_ref, *, tiles_per_seq, grouped):
    i = pl.program_id(0)
    j = pl.program_id(1)

    @pl.when(j == 0)
    def _():
        hs_ref[...] = _rmsnorm(x_ref[...], nw_ref[...]).astype(BF16)

    hs = hs_ref[...]
    gate_b = _dot(hs, wb_ref[...])
    u = _dot(hs, wc_ref[...]) * _dot(hs, wx_ref[...])
    u_ref[...] = u
    tm = u.shape[0]
    if grouped:
        state = st_ref[...]
        prev1 = _prev_rows_grouped(u, state, 1, CONV_A_WIDTH)
        prev2 = _prev_rows_grouped(u, state, 2, CONV_A_WIDTH)
    else:
        halo = jnp.where(i % tiles_per_seq != 0, carry_ref[j], 0.0)
        prev1 = _prev_rows_seq(u, halo, 1)
        prev2 = _prev_rows_seq(u, halo, 2)
        carry_ref[j] = u[tm - SUBLANES:tm]
    y = prev2 * cw_ref[0:1, :] + prev1 * cw_ref[1:2, :] + u * cw_ref[2:3, :]
    y_ref[...] = (gate_b * y).astype(BF16)


def _mixer_a(x, norm_w, w_in, conv_w, state, *, seq_len, grouped):
    t = x.shape[0]
    tm, tn = ROW_TILE, 512
    nj = D_MODEL // tn
    if grouped:
        st_spec = pl.BlockSpec((tm, tn), lambda i, j: (i, j))
    else:
        st_spec = pl.BlockSpec((SUBLANES, tn), lambda i, j: (0, j))
    kern = functools.partial(_mixer_a_kernel, tiles_per_seq=max(seq_len // tm, 1), grouped=grouped)
    return pl.pallas_call(
        kern,
        grid=(t // tm, nj),
        in_specs=[
            pl.BlockSpec((tm, D_MODEL), lambda i, j: (i, 0)),
            pl.BlockSpec((1, D_MODEL), lambda i, j: (0, 0)),
            pl.BlockSpec((D_MODEL, tn), lambda i, j: (0, j)),
            pl.BlockSpec((D_MODEL, tn), lambda i, j: (0, nj + j)),
            pl.BlockSpec((D_MODEL, tn), lambda i, j: (0, 2 * nj + j)),
            pl.BlockSpec((CONV_A_WIDTH, tn), lambda i, j: (0, j)),
            st_spec,
        ],
        out_specs=[
            pl.BlockSpec((tm, tn), lambda i, j: (i, j)),
            pl.BlockSpec((tm, tn), lambda i, j: (i, j)),
        ],
        out_shape=[
            jax.ShapeDtypeStruct((t, D_MODEL), F32),
            jax.ShapeDtypeStruct((t, D_MODEL), BF16),
        ],
        scratch_shapes=[
            pltpu.VMEM((tm, D_MODEL), BF16),
            pltpu.VMEM((nj, SUBLANES, tn), F32),
        ],
        compiler_params=_params(("arbitrary", "arbitrary")),
        name="mixer_a",
    )(x, norm_w, w_in, w_in, w_in, conv_w, state)


def _matmul_residual_kernel(a_ref, w_ref, r_ref, o_ref):
    o_ref[...] = r_ref[...] + _dot(a_ref[...], w_ref[...])


def _matmul_residual(a, w, res):
    t, k = a.shape
    n = w.shape[1]
    tm, tn = ROW_TILE, 512
    return pl.pallas_call(
        _matmul_residual_kernel,
        grid=(t // tm, n // tn),
        in_specs=[
            pl.BlockSpec((tm, k), lambda i, j: (i, 0)),
            pl.BlockSpec((k, tn), lambda i, j: (0, j)),
            pl.BlockSpec((tm, tn), lambda i, j: (i, j)),
        ],
        out_specs=pl.BlockSpec((tm, tn), lambda i, j: (i, j)),
        out_shape=jax.ShapeDtypeStruct((t, n), F32),
        compiler_params=_params(("parallel", "arbitrary")),
        name="matmul_residual",
    )(a, w, res)


def _mlp_kernel(x_ref, nw_ref, wu_ref, wd_ref, fw_ref, o_ref, hs_ref, *, final_norm):
    j = pl.program_id(1)

    @pl.when(j == 0)
    def _():
        x = x_ref[...]
        hs_ref[...] = _rmsnorm(x, nw_ref[...]).astype(BF16)
        o_ref[...] = x

    a = _dot(hs_ref[...], wu_ref[...])
    a = jnp.square(jnp.maximum(a, 0.0)).astype(BF16)
    o_ref[...] += _dot(a, wd_ref[...])

    if final_norm:
        @pl.when(j == pl.num_programs(1) - 1)
        def _():
            o_ref[...] = _rmsnorm(o_ref[...], fw_ref[...])


def _mlp(x, norm_w, w_up, w_down, final_w, *, layer, final_norm):
    t = x.shape[0]
    tm, tf = ROW_TILE, 512
    kern = functools.partial(_mlp_kernel, final_norm=final_norm)
    return pl.pallas_call(
        kern,
        grid=(t // tm, D_FF // tf),
        in_specs=[
            pl.BlockSpec((tm, D_MODEL), lambda i, j: (i, 0)),
            pl.BlockSpec((None, 1, D_MODEL), lambda i, j: (layer, 0, 0)),
            pl.BlockSpec((None, D_MODEL, tf), lambda i, j: (layer, 0, j)),
            pl.BlockSpec((None, tf, D_MODEL), lambda i, j: (layer, j, 0)),
            pl.BlockSpec((1, D_MODEL), lambda i, j: (0, 0)),
        ],
        out_specs=pl.BlockSpec((tm, D_MODEL), lambda i, j: (i, 0)),
        out_shape=jax.ShapeDtypeStruct((t, D_MODEL), F32),
        scratch_shapes=[pltpu.VMEM((tm, D_MODEL), BF16)],
        compiler_params=_params(("parallel", "arbitrary")),
        name="mlp",
    )(x, norm_w, w_up, w_down, final_w)


def _gdn_in_kernel(x_ref, nw_ref, w_ref, wg_ref, alog_ref, dtb_ref, qkvz_ref, gates_ref, hs_ref,
                   *, chunk):
    j = pl.program_id(1)

    @pl.when(j == 0)
    def _():
        hs = _rmsnorm(x_ref[...], nw_ref[...]).astype(BF16)
        hs_ref[...] = hs
        raw = _dot_nt(wg_ref[...], hs)
        tm = raw.shape[1]
        beta = _sigmoid(raw[0:V_HEADS])
        g = -jnp.exp(alog_ref[...]) * _softplus(raw[V_HEADS:2 * V_HEADS] + dtb_ref[...])
        pos = lax.broadcasted_iota(jnp.int32, g.shape, 1) & (chunk - 1)
        csum = g
        ssum = g
        s = 1
        while s < chunk:
            csum = csum + jnp.where(pos >= s, pltpu.roll(csum, s, 1), 0.0)
            ssum = ssum + jnp.where(pos < chunk - s, pltpu.roll(ssum, tm - s, 1), 0.0)
            s *= 2
        rest = ssum - g
        zero2 = jnp.zeros((2, tm), F32)
        for p in range(QK_HEADS):
            gates_ref[p, 0:2, :] = beta[2 * p:2 * p + 2]
            gates_ref[p, 2:4, :] = csum[2 * p:2 * p + 2]
            gates_ref[p, 4:6, :] = rest[2 * p:2 * p + 2]
            gates_ref[p, 6:8, :] = zero2

    qkvz_ref[...] = _dot_nt(hs_ref[...], w_ref[...])


def _gdn_in(x, norm_w, w_in_t, a_log_col, dt_bias_col, *, chunk):
    t = x.shape[0]
    tm, tn = ROW_TILE, 1024
    kern = functools.partial(_gdn_in_kernel, chunk=chunk)
    return pl.pallas_call(
        kern,
        grid=(t // tm, QKVZ_DIM // tn),
        in_specs=[
            pl.BlockSpec((tm, D_MODEL), lambda i, j: (i, 0)),
            pl.BlockSpec((1, D_MODEL), lambda i, j: (0, 0)),
            pl.BlockSpec((tn, D_MODEL), lambda i, j: (j, 0)),
            pl.BlockSpec((2 * V_HEADS, D_MODEL), lambda i, j: (QKVZ_DIM // (2 * V_HEADS), 0)),
            pl.BlockSpec((V_HEADS, 1), lambda i, j: (0, 0)),
            pl.BlockSpec((V_HEADS, 1), lambda i, j: (0, 0)),
        ],
        out_specs=[
            pl.BlockSpec((tm, tn), lambda i, j: (i, j)),
            pl.BlockSpec((QK_HEADS, SUBLANES, tm), lambda i, j: (0, 0, i)),
        ],
        out_shape=[
            jax.ShapeDtypeStruct((t, QKVZ_DIM), F32),
            jax.ShapeDtypeStruct((QK_HEADS, SUBLANES, t), F32),
        ],
        scratch_shapes=[pltpu.VMEM((tm, D_MODEL), BF16)],
        compiler_params=_params(("parallel", "arbitrary")),
        name="gdn_in",
    )(x, norm_w, w_in_t, w_in_t, a_log_col, dt_bias_col)


def _gdn_conv_kernel(x_ref, st_ref, cw_ref, o_ref, *, tiles_per_seq, grouped, tn):
    i = pl.program_id(0)
    j = pl.program_id(1)
    tm = x_ref.shape[0]
    n_qk = 2 * KEY_DIM // tn

    def run(normalise):
        scale = jnp.where(j < KEY_DIM // tn, HEAD_DIM ** -0.5, 1.0).astype(F32)
        for hd in range(tn // HEAD_DIM):
            cols = slice(hd * HEAD_DIM, (hd + 1) * HEAD_DIM)
            taps = [cw_ref[s:s + 1, cols] for s in range(CONV_B_WIDTH)]

            def finish(u, prev, rows):
                y = prev[2] * taps[0] + prev[1] * taps[1] + prev[0] * taps[2] + u * taps[3]
                o_ref[rows, cols] = y * _sigmoid(y)

            def strip(rb, carry):
                r0 = pl.multiple_of(rb * CONV_STRIP, CONV_STRIP)
                rows = pl.ds(r0, CONV_STRIP)
                u = x_ref[rows, cols]
                if grouped:
                    state = st_ref[rows, cols]
                    prev = [_prev_rows_grouped(u, state, s, CONV_B_WIDTH) for s in (1, 2, 3)]
                else:
                    halo = x_ref[pl.ds(pl.multiple_of(r0 - SUBLANES, SUBLANES), SUBLANES), cols]
                    prev = [_prev_rows_seq(u, halo, s) for s in (1, 2, 3)]
                finish(u, prev, rows)
                return carry

            first = 0
            if not grouped:
                u = x_ref[0:CONV_STRIP, cols]
                halo = jnp.where(i % tiles_per_seq != 0, st_ref[:, cols], 0.0)
                finish(u, [_prev_rows_seq(u, halo, s) for s in (1, 2, 3)], slice(0, CONV_STRIP))
                first = 1
            lax.fori_loop(first, tm // CONV_STRIP, strip, 0, unroll=4)
            if normalise:
                y = o_ref[:, cols]
                ss = jnp.sum(y * y, axis=-1, keepdims=True)
                o_ref[:, cols] = (y * lax.rsqrt(ss + EPS)) * scale

    @pl.when(j >= n_qk)
    def _():
        run(False)

    @pl.when(j < n_qk)
    def _():
        run(True)


def _gdn_conv(qkvz, state, conv_w, *, seq_len, grouped):
    t = qkvz.shape[0]
    tm, tn = CONV_TILE, 1024
    if grouped:
        st_spec = pl.BlockSpec((tm, tn), lambda i, j: (i, j))
        st = state
    else:
        blocks_per_tile = tm // SUBLANES
        st_spec = pl.BlockSpec((SUBLANES, tn), lambda i, j: (jnp.maximum(i * blocks_per_tile - 1, 0), j))
        st = qkvz
    kern = functools.partial(_gdn_conv_kernel, tiles_per_seq=max(seq_len // tm, 1), grouped=grouped, tn=tn)
    return pl.pallas_call(
        kern,
        grid=(t // tm, CONV_DIM // tn),
        in_specs=[
            pl.BlockSpec((tm, tn), lambda i, j: (i, j)),
            st_spec,
            pl.BlockSpec((CONV_B_WIDTH, tn), lambda i, j: (0, j)),
        ],
        out_specs=pl.BlockSpec((tm, tn), lambda i, j: (i, j)),
        out_shape=jax.ShapeDtypeStruct((t, CONV_DIM), F32),
        compiler_params=_params(("parallel", "parallel")),
        name="gdn_conv",
    )(qkvz, st, conv_w)


def _block_masks(rows, chunk):
    levels = chunk.bit_length() - 1
    ri = lax.broadcasted_iota(jnp.int32, (rows, rows), 0)
    ci = lax.broadcasted_iota(jnp.int32, (rows, rows), 1)
    same = [None] + [lax.shift_right_logical(ri, lb) == lax.shift_right_logical(ci, lb)
                     for lb in range(1, levels + 1)]
    return {
        "diag": ri == ci,
        "pair": same[1],
        "join": [same[lb + 1] & ~same[lb] for lb in range(1, levels)],
        "low": same[levels] & (ri >= ci),
        "strict": same[levels] & (ri > ci),
    }


def _gate_cols(g):
    rows = g.shape[1]
    padded = jnp.concatenate([g, jnp.zeros((HEAD_DIM - SUBLANES, rows), F32)], axis=0)
    return padded.T


def _gdn_chunk_parts(q_ref, k_ref, v_ref, gates, masks):
    heads = []
    for pp, grow in enumerate(gates):
        qcols = slice(pp * HEAD_DIM, (pp + 1) * HEAD_DIM)
        q = q_ref[:, qcols]
        k = k_ref[:, qcols]
        kb = k.astype(BF16)
        kk = _dot_nt(kb, kb)
        qk = _dot_nt(q.astype(BF16), kb)
        cols = _gate_cols(grow)
        for hh in range(2):
            hv = 2 * pp + hh
            beta_c = cols[:, hh:hh + 1]
            g_c = cols[:, 2 + hh:3 + hh]
            x_c = cols[:, 4 + hh:5 + hh]
            g_r = grow[2 + hh:3 + hh, :]
            decay = jnp.exp(jnp.where(masks["low"], g_c - g_r, -jnp.inf))
            eg = jnp.exp(g_c)
            v = v_ref[:, hv * HEAD_DIM:(hv + 1) * HEAD_DIM]
            heads.append({
                "l_mat": jnp.where(masks["strict"], beta_c * kk * decay, 0.0).astype(BF16),
                "rhs": jnp.concatenate([v * beta_c, k * (beta_c * eg)], axis=1).astype(BF16),
                "qk_m": (qk * decay).astype(BF16),
                "q_dec": q * eg,
                "k_dec": k * jnp.exp(x_c),
                "g_tot": g_c + x_c,
            })
    invs = [jnp.where(masks["diag"], 1.0, jnp.where(masks["pair"], -h["l_mat"], 0.0)) for h in heads]
    for join in masks["join"]:
        cross = [_dot(jnp.where(join, h["l_mat"], 0.0), inv).astype(BF16)
                 for h, inv in zip(heads, invs)]
        invs = [inv - _dot(inv, cr).astype(BF16) for inv, cr in zip(invs, cross)]
    for h, inv in zip(heads, invs):
        uw = _dot(inv, h["rhs"])
        h["u_mat"] = uw[:, :HEAD_DIM]
        h["w_mat"] = uw[:, HEAD_DIM:]
    return heads


def _gated_out(o, z, onorm):
    ms = jnp.mean(o * o, axis=-1, keepdims=True)
    return ((o * lax.rsqrt(ms + EPS)) * onorm) * (z * _sigmoid(z))


def _gdn_prompt_kernel(q_ref, k_ref, v_ref, z_ref, g_ref, on_ref, o_ref, s_ref, *, chunk, pairs):
    grp = pl.program_id(2)

    @pl.when(grp == 0)
    def _():
        s_ref[...] = jnp.zeros(s_ref.shape, F32)

    rows = q_ref.shape[0]
    masks = _block_masks(rows, chunk)
    heads = _gdn_chunk_parts(q_ref, k_ref, v_ref, [g_ref[pp] for pp in range(pairs)], masks)
    nh = len(heads)
    wbs = [h["w_mat"].astype(BF16) for h in heads]
    qdbs = [h["q_dec"].astype(BF16) for h in heads]
    kdbs = [h["k_dec"].astype(BF16) for h in heads]
    states = [s_ref[hv] for hv in range(nh)]
    v_new = [[] for _ in range(nh)]
    q_state = [[] for _ in range(nh)]
    for n in range(rows // chunk):
        rs = slice(n * chunk, (n + 1) * chunk)
        res = [_dot(jnp.concatenate([wb[rs], qdb[rs]], axis=0), st.astype(BF16))
               for wb, qdb, st in zip(wbs, qdbs, states)]
        for hv, h in enumerate(heads):
            v_n = h["u_mat"][rs] - res[hv][0:chunk]
            v_new[hv].append(v_n)
            q_state[hv].append(res[hv][chunk:2 * chunk])
            g_last = jnp.exp(h["g_tot"][n * chunk:n * chunk + 1, :])
            states[hv] = g_last * states[hv] + _dot_tn(kdbs[hv][rs], v_n.astype(BF16))
    onorm = on_ref[...]
    for hv, h in enumerate(heads):
        hcols = slice(hv * HEAD_DIM, (hv + 1) * HEAD_DIM)
        s_ref[hv] = states[hv]
        o = (jnp.concatenate(q_state[hv], axis=0)
             + _dot(h["qk_m"], jnp.concatenate(v_new[hv], axis=0).astype(BF16)))
        o_ref[:, hcols] = _gated_out(o, z_ref[:, hcols], onorm).astype(BF16)


def _gdn_prompt(qkv, qkvz, gates, o_norm, *, batch, seq_len):
    t = qkv.shape[0]
    r = GDN_ROWS
    groups = seq_len // r
    pairs = GDN_PAIRS
    hblocks = QK_HEADS // pairs
    qw, vw = pairs * HEAD_DIM, 2 * pairs * HEAD_DIM
    kern = functools.partial(_gdn_prompt_kernel, chunk=PROMPT_CHUNK, pairs=pairs)
    row = lambda b, h, g: b * groups + g
    return pl.pallas_call(
        kern,
        grid=(batch, hblocks, groups),
        in_specs=[
            pl.BlockSpec((r, qw), lambda b, h, g: (row(b, h, g), h)),
            pl.BlockSpec((r, qw), lambda b, h, g: (row(b, h, g), hblocks + h)),
            pl.BlockSpec((r, vw), lambda b, h, g: (row(b, h, g), hblocks + h)),
            pl.BlockSpec((r, vw), lambda b, h, g: (row(b, h, g), 2 * hblocks + h)),
            pl.BlockSpec((pairs, SUBLANES, r), lambda b, h, g: (h, 0, row(b, h, g))),
            pl.BlockSpec((1, HEAD_DIM), lambda b, h, g: (0, 0)),
        ],
        out_specs=[
            pl.BlockSpec((r, vw), lambda b, h, g: (row(b, h, g), h)),
            pl.BlockSpec((None, 2 * pairs, HEAD_DIM, HEAD_DIM), lambda b, h, g: (b, h, 0, 0)),
        ],
        out_shape=[
            jax.ShapeDtypeStruct((t, VAL_DIM), BF16),
            jax.ShapeDtypeStruct((batch, V_HEADS, HEAD_DIM, HEAD_DIM), F32),
        ],
        compiler_params=_params(("parallel", "parallel", "arbitrary")),
        name="gdn_prompt",
    )(qkv, qkv, qkv, qkvz, gates, o_norm)


def _gdn_sample_kernel(q_ref, k_ref, v_ref, z_ref, g_ref, on_ref, s0_ref, o_ref, s_ref,
                       lhs_scr, u_scr, kd_scr, gl_scr, vn_scr, os_scr, *, chunk):
    rows = q_ref.shape[0]
    nseq = rows // chunk
    masks = _block_masks(rows, chunk)
    heads = _gdn_chunk_parts(q_ref, k_ref, v_ref, [g_ref[...]], masks)
    for hh, h in enumerate(heads):
        lhs_scr[hh, :, 0:chunk, :] = h["w_mat"].reshape(nseq, chunk, HEAD_DIM)
        lhs_scr[hh, :, chunk:2 * chunk, :] = h["q_dec"].reshape(nseq, chunk, HEAD_DIM)
        u_scr[hh] = h["u_mat"]
        kd_scr[hh] = h["k_dec"]
        gl_scr[hh] = jnp.broadcast_to(jnp.exp(h["g_tot"]), (rows, HEAD_DIM))
    qk_masked = [h["qk_m"] for h in heads]
    onorm = on_ref[...]

    def per_sequences(step, carry):
        work = [(step * SEQ_UNROLL + lane, hh) for lane in range(SEQ_UNROLL) for hh in range(2)]
        rws = [pl.ds(pl.multiple_of(s * chunk, chunk), chunk) for s, _ in work]
        s0s = [s0_ref[s, hh] for s, hh in work]
        res = [_dot(lhs_scr[hh, s].astype(BF16), s0.astype(BF16))
               for (s, hh), s0 in zip(work, s0s)]
        v_new = [u_scr[hh, rw, :] - r[0:chunk] for (_, hh), rw, r in zip(work, rws, res)]
        upd = [_dot_tn(kd_scr[hh, rw, :], vn) for (_, hh), rw, vn in zip(work, rws, v_new)]
        for (s, hh), rw, s0, r, vn, up in zip(work, rws, s0s, res, v_new, upd):
            vn_scr[hh, rw, :] = vn
            os_scr[hh, rw, :] = r[chunk:2 * chunk]
            s_ref[s, hh] = gl_scr[hh, rw, :][0:1, :] * s0 + up
        return carry

    lax.fori_loop(0, nseq // SEQ_UNROLL, per_sequences, 0)
    for hh in range(2):
        hcols = slice(hh * HEAD_DIM, (hh + 1) * HEAD_DIM)
        o = os_scr[hh] + _dot(qk_masked[hh], vn_scr[hh].astype(BF16))
        o_ref[:, hcols] = _gated_out(o, z_ref[:, hcols], onorm).astype(BF16)


def _gdn_sample(qkv, qkvz, gates, o_norm, s0, *, chunk):
    t = qkv.shape[0]
    r = GDN_ROWS
    nseq = r // chunk
    kern = functools.partial(_gdn_sample_kernel, chunk=chunk)
    return pl.pallas_call(
        kern,
        grid=(t // r, QK_HEADS),
        in_specs=[
            pl.BlockSpec((r, HEAD_DIM), lambda i, h: (i, h)),
            pl.BlockSpec((r, HEAD_DIM), lambda i, h: (i, QK_HEADS + h)),
            pl.BlockSpec((r, 2 * HEAD_DIM), lambda i, h: (i, QK_HEADS + h)),
            pl.BlockSpec((r, 2 * HEAD_DIM), lambda i, h: (i, 2 * QK_HEADS + h)),
            pl.BlockSpec((None, SUBLANES, r), lambda i, h: (h, 0, i)),
            pl.BlockSpec((1, HEAD_DIM), lambda i, h: (0, 0)),
            pl.BlockSpec((nseq, 2, HEAD_DIM, HEAD_DIM), lambda i, h: (i, h, 0, 0)),
        ],
        out_specs=[
            pl.BlockSpec((r, 2 * HEAD_DIM), lambda i, h: (i, h)),
            pl.BlockSpec((nseq, 2, HEAD_DIM, HEAD_DIM), lambda i, h: (i, h, 0, 0)),
        ],
        out_shape=[
            jax.ShapeDtypeStruct((t, VAL_DIM), BF16),
            jax.ShapeDtypeStruct(s0.shape, F32),
        ],
        scratch_shapes=[
            pltpu.VMEM((2, nseq, 2 * chunk, HEAD_DIM), F32),
            pltpu.VMEM((2, r, HEAD_DIM), F32),
            pltpu.VMEM((2, r, HEAD_DIM), F32),
            pltpu.VMEM((2, r, HEAD_DIM), F32),
            pltpu.VMEM((2, r, HEAD_DIM), F32),
            pltpu.VMEM((2, r, HEAD_DIM), F32),
        ],
        compiler_params=_params(("parallel", "parallel")),
        name="gdn_sample",
    )(qkv, qkv, qkv, qkvz, gates, o_norm, s0)


def _front_pad_rows(state, rows):
    b, w, c = state.shape
    return jnp.pad(state, ((0, 0), (0, rows - w), (0, 0))).reshape(b * rows, c)


def _trunk(x, conv_a, conv_b, ssm_b, wts, *, grouped):
    batch, seq_len, _ = x.shape
    t = batch * seq_len
    x0 = x.reshape(t, D_MODEL)
    if grouped:
        state_a = _front_pad_rows(conv_a, SUBLANES)
        state_b = _front_pad_rows(conv_b, SUBLANES)
        chunk = seq_len
    else:
        state_a = jnp.zeros((SUBLANES, D_MODEL), F32)
        state_b = None
        chunk = PROMPT_CHUNK

    u, y = _mixer_a(x0, wts["norm_a"], wts["w_in_a"], wts["w_conv_a"], state_a,
                    seq_len=seq_len, grouped=grouped)
    new_conv_a = u.reshape(batch, seq_len, D_MODEL)[:, seq_len - (CONV_A_WIDTH - 1):]
    x1 = _matmul_residual(y, wts["w_out_a"], x0)
    x2 = _mlp(x1, wts["mlp_norm"], wts["w_up"], wts["w_down"], wts["final_norm"], layer=0,
              final_norm=False)

    qkvz, gates = _gdn_in(x2, wts["norm_b"], wts["w_in_b_t"], wts["a_log"], wts["dt_bias"],
                          chunk=chunk)
    new_conv_b = qkvz.reshape(batch, seq_len, QKVZ_DIM)[:, seq_len - (CONV_B_WIDTH - 1):, :CONV_DIM]
    qkv = _gdn_conv(qkvz, state_b, wts["w_conv_b"], seq_len=seq_len, grouped=grouped)
    if grouped:
        o, s_new = _gdn_sample(qkv, qkvz, gates, wts["o_norm"], ssm_b, chunk=chunk)
    else:
        o, s_new = _gdn_prompt(qkv, qkvz, gates, wts["o_norm"], batch=batch, seq_len=seq_len)
    x3 = _matmul_residual(o, wts["w_out_b"], x2)
    x4 = _mlp(x3, wts["mlp_norm"], wts["w_up"], wts["w_down"], wts["final_norm"], layer=1,
              final_norm=True)
    return (x4.reshape(batch, seq_len, D_MODEL), new_conv_a[None], new_conv_b[None], s_new[None])


def kernel(x_prompt, x_sample, state_conv_a, state_conv_b, state_ssm_b, norm_a, w_in_a, w_conv_a,
           w_out_a, norm_b, w_in_b, w_conv_b, a_log_b, dt_bias_b, o_norm_b, w_out_b, mlp_norm,
           w_up, w_down, final_norm):
    wts = {
        "norm_a": norm_a[0][None],
        "w_in_a": w_in_a[0].astype(BF16),
        "w_conv_a": w_conv_a[0],
        "w_out_a": w_out_a[0].astype(BF16),
        "norm_b": norm_b[0][None],
        "w_in_b_t": w_in_b[0].T.astype(BF16),
        "w_conv_b": w_conv_b[0],
        "a_log": a_log_b[0][:, None],
        "dt_bias": dt_bias_b[0][:, None],
        "o_norm": o_norm_b[0][None],
        "w_out_b": w_out_b[0].astype(BF16),
        "mlp_norm": mlp_norm[:, None, :],
        "w_up": w_up.astype(BF16),
        "w_down": w_down.astype(BF16),
        "final_norm": final_norm[None],
    }
    y_p, ca_p, cb_p, s_p = _trunk(x_prompt, None, None, None, wts, grouped=False)
    y_s, ca_s, cb_s, s_s = _trunk(x_sample, state_conv_a[0], state_conv_b[0], state_ssm_b[0], wts,
                                  grouped=True)
    return (y_p, y_s, ca_p, cb_p, s_p, ca_s, cb_s, s_s)
```

```python
import functools

import jax
import jax.numpy as jnp
from jax import lax
from jax.experimental import pallas as pl
from jax.experimental.pallas import tpu as pltpu

D_MODEL = 2048
D_FF = 4 * D_MODEL
HEAD_DIM = 128
QK_HEADS = D_MODEL // HEAD_DIM
V_HEADS = 2 * QK_HEADS
KEY_DIM = QK_HEADS * HEAD_DIM
VAL_DIM = V_HEADS * HEAD_DIM
CONV_DIM = 2 * KEY_DIM + VAL_DIM
QKVZ_DIM = CONV_DIM + VAL_DIM
CONV_A_WIDTH = 3
CONV_B_WIDTH = 4
PROMPT_CHUNK = 64
EPS = 1e-6

F32 = jnp.float32
BF16 = jnp.bfloat16

SUBLANES = 8
MXU_COLS = 256
ROW_TILE = 1024
CONV_TILE = 512
CONV_STRIP = 32
GDN_ROWS = 256
GDN_PAIRS = 8
SEQ_UNROLL = 16
VMEM_LIMIT = 56 * 1024 * 1024


def _params(semantics):
    return pltpu.CompilerParams(dimension_semantics=semantics, vmem_limit_bytes=VMEM_LIMIT)


def _rmsnorm(x, w):
    ms = jnp.mean(x * x, axis=-1, keepdims=True)
    return (x * lax.rsqrt(ms + EPS)) * w


def _dot(a, b):
    return jnp.dot(a, b, preferred_element_type=F32)


def _dot_nt(a, b):
    return lax.dot_general(a, b, (((1,), (1,)), ((), ())), preferred_element_type=F32)


def _dot_tn(a, b):
    return lax.dot_general(a, b, (((0,), (0,)), ((), ())), preferred_element_type=F32)


def _sigmoid(x):
    return 1.0 / (1.0 + jnp.exp(-x))


def _softplus(x):
    return jnp.maximum(x, 0.0) + jnp.log1p(jnp.exp(-jnp.abs(x)))


def _prev_rows_seq(u, halo, j):
    tm = u.shape[0]
    return pltpu.roll(jnp.concatenate([u, halo], axis=0), j, 0)[0:tm]


def _state_rows_grouped(shifted, state, j, width):
    tm = shifted.shape[0]
    rmod = lax.broadcasted_iota(jnp.int32, shifted.shape, 0) & (SUBLANES - 1)
    back = width - 1 - j
    st = state if back == 0 else pltpu.roll(state, tm - back, 0)
    return jnp.where(rmod < j, st, shifted)


def _prev_rows_grouped(u, state, j, width):
    return _state_rows_grouped(pltpu.roll(u, j, 0), state, j, width)


def _mixer_a_kernel(x_ref, nw_ref, wb_ref, wc_ref, wx_ref, cw_ref, st_ref, u_ref, y_ref,
                    hs_ref, carry_ref, *, tiles_per_seq, grouped):
    i = pl.program_id(0)
    j = pl.program_id(1)

    @pl.when(j == 0)
    def _():
        hs_ref[...] = _rmsnorm(x_ref[...], nw_ref[...]).astype(BF16)

    hs = hs_ref[...]
    tm = hs.shape[0]
    for c in range(wb_ref.shape[1] // MXU_COLS):
        cols = slice(c * MXU_COLS, (c + 1) * MXU_COLS)
        gate_b = _dot(hs, wb_ref[:, cols])
        u = _dot(hs, wc_ref[:, cols]) * _dot(hs, wx_ref[:, cols])
        u_ref[:, cols] = u
        if grouped:
            state = st_ref[:, cols]
            prev1 = _prev_rows_grouped(u, state, 1, CONV_A_WIDTH)
            prev2 = _prev_rows_grouped(u, state, 2, CONV_A_WIDTH)
        else:
            halo = jnp.where(i % tiles_per_seq != 0, carry_ref[j, :, cols], 0.0)
            prev1 = _prev_rows_seq(u, halo, 1)
            prev2 = _prev_rows_seq(u, halo, 2)
            carry_ref[j, :, cols] = u[tm - SUBLANES:tm]
        y = prev2 * cw_ref[0:1, cols] + prev1 * cw_ref[1:2, cols] + u * cw_ref[2:3, cols]
        y_ref[:, cols] = (gate_b * y).astype(BF16)


def _mixer_a(x, norm_w, w_in, conv_w, state, *, seq_len, grouped):
    t = x.shape[0]
    tm, tn = ROW_TILE, 512
    nj = D_MODEL // tn
    if grouped:
        st_spec = pl.BlockSpec((tm, tn), lambda i, j: (i, j))
    else:
        st_spec = pl.BlockSpec((SUBLANES, tn), lambda i, j: (0, j))
    kern = functools.partial(_mixer_a_kernel, tiles_per_seq=max(seq_len // tm, 1), grouped=grouped)
    return pl.pallas_call(
        kern,
        grid=(t // tm, nj),
        in_specs=[
            pl.BlockSpec((tm, D_MODEL), lambda i, j: (i, 0)),
            pl.BlockSpec((1, D_MODEL), lambda i, j: (0, 0)),
            pl.BlockSpec((D_MODEL, tn), lambda i, j: (0, j)),
            pl.BlockSpec((D_MODEL, tn), lambda i, j: (0, nj + j)),
            pl.BlockSpec((D_MODEL, tn), lambda i, j: (0, 2 * nj + j)),
            pl.BlockSpec((CONV_A_WIDTH, tn), lambda i, j: (0, j)),
            st_spec,
        ],
        out_specs=[
            pl.BlockSpec((tm, tn), lambda i, j: (i, j)),
            pl.BlockSpec((tm, tn), lambda i, j: (i, j)),
        ],
        out_shape=[
            jax.ShapeDtypeStruct((t, D_MODEL), F32),
            jax.ShapeDtypeStruct((t, D_MODEL), BF16),
        ],
        scratch_shapes=[
            pltpu.VMEM((tm, D_MODEL), BF16),
            pltpu.VMEM((nj, SUBLANES, tn), F32),
        ],
        compiler_params=_params(("arbitrary", "arbitrary")),
        name="mixer_a",
    )(x, norm_w, w_in, w_in, w_in, conv_w, state)


def _matmul_residual_kernel(a_ref, w_ref, r_ref, o_ref):
    o_ref[...] = r_ref[...] + _dot(a_ref[...], w_ref[...])


def _matmul_residual(a, w, res):
    t, k = a.shape
    n = w.shape[1]
    tm, tn = ROW_TILE, 512
    return pl.pallas_call(
        _matmul_residual_kernel,
        grid=(t // tm, n // tn),
        in_specs=[
            pl.BlockSpec((tm, k), lambda i, j: (i, 0)),
            pl.BlockSpec((k, tn), lambda i, j: (0, j)),
            pl.BlockSpec((tm, tn), lambda i, j: (i, j)),
        ],
        out_specs=pl.BlockSpec((tm, tn), lambda i, j: (i, j)),
        out_shape=jax.ShapeDtypeStruct((t, n), F32),
        compiler_params=_params(("parallel", "arbitrary")),
        name="matmul_residual",
    )(a, w, res)


def _mlp_kernel(x_ref, nw_ref, wu_ref, wd_ref, fw_ref, o_ref, hs_ref, *, final_norm):
    j = pl.program_id(1)

    @pl.when(j == 0)
    def _():
        x = x_ref[...]
        hs_ref[...] = _rmsnorm(x, nw_ref[...]).astype(BF16)
        o_ref[...] = x

    a = _dot(hs_ref[...], wu_ref[...])
    a = jnp.square(jnp.maximum(a, 0.0)).astype(BF16)
    o_ref[...] += _dot(a, wd_ref[...])

    if final_norm:
        @pl.when(j == pl.num_programs(1) - 1)
        def _():
            o_ref[...] = _rmsnorm(o_ref[...], fw_ref[...])


def _mlp(x, norm_w, w_up, w_down, final_w, *, layer, final_norm):
    t = x.shape[0]
    tm, tf = ROW_TILE, 512
    kern = functools.partial(_mlp_kernel, final_norm=final_norm)
    return pl.pallas_call(
        kern,
        grid=(t // tm, D_FF // tf),
        in_specs=[
            pl.BlockSpec((tm, D_MODEL), lambda i, j: (i, 0)),
            pl.BlockSpec((None, 1, D_MODEL), lambda i, j: (layer, 0, 0)),
            pl.BlockSpec((None, D_MODEL, tf), lambda i, j: (layer, 0, j)),
            pl.BlockSpec((None, tf, D_MODEL), lambda i, j: (layer, j, 0)),
            pl.BlockSpec((1, D_MODEL), lambda i, j: (0, 0)),
        ],
        out_specs=pl.BlockSpec((tm, D_MODEL), lambda i, j: (i, 0)),
        out_shape=jax.ShapeDtypeStruct((t, D_MODEL), F32),
        scratch_shapes=[pltpu.VMEM((tm, D_MODEL), BF16)],
        compiler_params=_params(("parallel", "arbitrary")),
        name="mlp",
    )(x, norm_w, w_up, w_down, final_w)


def _gdn_in_kernel(x_ref, nw_ref, w_ref, wg_ref, alog_ref, dtb_ref, qkvz_ref, gates_ref, hs_ref,
                   *, chunk):
    j = pl.program_id(1)

    @pl.when(j == 0)
    def _():
        hs = _rmsnorm(x_ref[...], nw_ref[...]).astype(BF16)
        hs_ref[...] = hs
        raw = _dot_nt(wg_ref[...], hs)
        tm = raw.shape[1]
        beta = _sigmoid(raw[0:V_HEADS])
        g = -jnp.exp(alog_ref[...]) * _softplus(raw[V_HEADS:2 * V_HEADS] + dtb_ref[...])
        pos = lax.broadcasted_iota(jnp.int32, g.shape, 1) & (chunk - 1)
        csum = g
        ssum = g
        s = 1
        while s < chunk:
            csum = csum + jnp.where(pos >= s, pltpu.roll(csum, s, 1), 0.0)
            ssum = ssum + jnp.where(pos < chunk - s, pltpu.roll(ssum, tm - s, 1), 0.0)
            s *= 2
        rest = ssum - g
        zero2 = jnp.zeros((2, tm), F32)
        for p in range(QK_HEADS):
            gates_ref[p, 0:2, :] = beta[2 * p:2 * p + 2]
            gates_ref[p, 2:4, :] = csum[2 * p:2 * p + 2]
            gates_ref[p, 4:6, :] = rest[2 * p:2 * p + 2]
            gates_ref[p, 6:8, :] = zero2

    qkvz_ref[...] = _dot_nt(hs_ref[...], w_ref[...])


def _gdn_in(x, norm_w, w_in_t, a_log_col, dt_bias_col, *, chunk):
    t = x.shape[0]
    tm, tn = ROW_TILE, 1024
    kern = functools.partial(_gdn_in_kernel, chunk=chunk)
    return pl.pallas_call(
        kern,
        grid=(t // tm, QKVZ_DIM // tn),
        in_specs=[
            pl.BlockSpec((tm, D_MODEL), lambda i, j: (i, 0)),
            pl.BlockSpec((1, D_MODEL), lambda i, j: (0, 0)),
            pl.BlockSpec((tn, D_MODEL), lambda i, j: (j, 0)),
            pl.BlockSpec((2 * V_HEADS, D_MODEL), lambda i, j: (QKVZ_DIM // (2 * V_HEADS), 0)),
            pl.BlockSpec((V_HEADS, 1), lambda i, j: (0, 0)),
            pl.BlockSpec((V_HEADS, 1), lambda i, j: (0, 0)),
        ],
        out_specs=[
            pl.BlockSpec((tm, tn), lambda i, j: (i, j)),
            pl.BlockSpec((QK_HEADS, SUBLANES, tm), lambda i, j: (0, 0, i)),
        ],
        out_shape=[
            jax.ShapeDtypeStruct((t, QKVZ_DIM), F32),
            jax.ShapeDtypeStruct((QK_HEADS, SUBLANES, t), F32),
        ],
        scratch_shapes=[pltpu.VMEM((tm, D_MODEL), BF16)],
        compiler_params=_params(("parallel", "arbitrary")),
        name="gdn_in",
    )(x, norm_w, w_in_t, w_in_t, a_log_col, dt_bias_col)


def _gdn_conv_kernel(x_ref, st_ref, cw_ref, o_ref, slab_ref, *, tiles_per_seq, grouped, tn):
    i = pl.program_id(0)
    j = pl.program_id(1)
    tm = x_ref.shape[0]
    n_qk = 2 * KEY_DIM // tn

    def run(normalise):
        scale = jnp.where(j < KEY_DIM // tn, HEAD_DIM ** -0.5, 1.0).astype(F32)
        for hd in range(tn // HEAD_DIM):
            cols = slice(hd * HEAD_DIM, (hd + 1) * HEAD_DIM)
            taps = [cw_ref[s:s + 1, cols] for s in range(CONV_B_WIDTH)]
            if grouped:
                slab_ref[0:SUBLANES, :] = jnp.zeros((SUBLANES, HEAD_DIM), F32)
            else:
                slab_ref[0:SUBLANES, :] = jnp.where(i % tiles_per_seq != 0, st_ref[:, cols], 0.0)
            slab_ref[SUBLANES:SUBLANES + tm, :] = x_ref[:, cols]

            def strip(rb, carry):
                r0 = pl.multiple_of(rb * CONV_STRIP, CONV_STRIP)
                u = slab_ref[pl.ds(r0 + SUBLANES, CONV_STRIP), :]
                prev = [slab_ref[pl.ds(r0 + SUBLANES - s, CONV_STRIP), :] for s in (1, 2, 3)]
                if grouped:
                    state = st_ref[pl.ds(r0, CONV_STRIP), cols]
                    prev = [_state_rows_grouped(prev[s - 1], state, s, CONV_B_WIDTH) for s in (1, 2, 3)]
                y = prev[2] * taps[0] + prev[1] * taps[1] + prev[0] * taps[2] + u * taps[3]
                o_ref[pl.ds(r0, CONV_STRIP), cols] = y * _sigmoid(y)
                return carry

            lax.fori_loop(0, tm // CONV_STRIP, strip, 0, unroll=True)
            if normalise:
                y = o_ref[:, cols]
                ss = jnp.sum(y * y, axis=-1, keepdims=True)
                o_ref[:, cols] = (y * lax.rsqrt(ss + EPS)) * scale

    @pl.when(j >= n_qk)
    def _():
        run(False)

    @pl.when(j < n_qk)
    def _():
        run(True)


def _gdn_conv(qkvz, state, conv_w, *, seq_len, grouped):
    t = qkvz.shape[0]
    tm, tn = CONV_TILE, 1024
    if grouped:
        st_spec = pl.BlockSpec((tm, tn), lambda i, j: (i, j))
        st = state
    else:
        blocks_per_tile = tm // SUBLANES
        st_spec = pl.BlockSpec((SUBLANES, tn), lambda i, j: (jnp.maximum(i * blocks_per_tile - 1, 0), j))
        st = qkvz
    kern = functools.partial(_gdn_conv_kernel, tiles_per_seq=max(seq_len // tm, 1), grouped=grouped, tn=tn)
    return pl.pallas_call(
        kern,
        grid=(t // tm, CONV_DIM // tn),
        in_specs=[
            pl.BlockSpec((tm, tn), lambda i, j: (i, j)),
            st_spec,
            pl.BlockSpec((CONV_B_WIDTH, tn), lambda i, j: (0, j)),
        ],
        out_specs=pl.BlockSpec((tm, tn), lambda i, j: (i, j)),
        out_shape=jax.ShapeDtypeStruct((t, CONV_DIM), F32),
        scratch_shapes=[pltpu.VMEM((SUBLANES + tm, HEAD_DIM), F32)],
        compiler_params=_params(("parallel", "parallel")),
        name="gdn_conv",
    )(qkvz, st, conv_w)


def _block_masks(rows, chunk):
    levels = chunk.bit_length() - 1
    ri = lax.broadcasted_iota(jnp.int32, (rows, rows), 0)
    ci = lax.broadcasted_iota(jnp.int32, (rows, rows), 1)
    same = [None] + [lax.shift_right_logical(ri, lb) == lax.shift_right_logical(ci, lb)
                     for lb in range(1, levels + 1)]
    return {
        "diag": ri == ci,
        "pair": same[1],
        "join": [same[lb + 1] & ~same[lb] for lb in range(1, levels)],
        "low": same[levels] & (ri >= ci),
        "strict": same[levels] & (ri > ci),
    }


def _gate_cols(g):
    rows = g.shape[1]
    padded = jnp.concatenate([g, jnp.zeros((HEAD_DIM - SUBLANES, rows), F32)], axis=0)
    return padded.T


def _gdn_chunk_parts(q_ref, k_ref, v_ref, gates, masks):
    heads = []
    for pp, grow in enumerate(gates):
        qcols = slice(pp * HEAD_DIM, (pp + 1) * HEAD_DIM)
        q = q_ref[:, qcols]
        k = k_ref[:, qcols]
        kb = k.astype(BF16)
        kk = _dot_nt(kb, kb)
        qk = _dot_nt(q.astype(BF16), kb)
        cols = _gate_cols(grow)
        for hh in range(2):
            hv = 2 * pp + hh
            beta_c = cols[:, hh:hh + 1]
            g_c = cols[:, 2 + hh:3 + hh]
            x_c = cols[:, 4 + hh:5 + hh]
            g_r = grow[2 + hh:3 + hh, :]
            decay = jnp.exp(jnp.where(masks["low"], g_c - g_r, -jnp.inf))
            eg = jnp.exp(g_c)
            v = v_ref[:, hv * HEAD_DIM:(hv + 1) * HEAD_DIM]
            heads.append({
                "l_mat": jnp.where(masks["strict"], beta_c * kk * decay, 0.0).astype(BF16),
                "rhs": jnp.concatenate([v * beta_c, k * (beta_c * eg)], axis=1).astype(BF16),
                "qk_m": (qk * decay).astype(BF16),
                "q_dec": q * eg,
                "k_dec": k * jnp.exp(x_c),
                "g_tot": g_c + x_c,
            })
    invs = [jnp.where(masks["diag"], 1.0, jnp.where(masks["pair"], -h["l_mat"], 0.0)) for h in heads]
    for join in masks["join"]:
        cross = [_dot(jnp.where(join, h["l_mat"], 0.0), inv).astype(BF16)
                 for h, inv in zip(heads, invs)]
        invs = [inv - _dot(inv, cr).astype(BF16) for inv, cr in zip(invs, cross)]
    for h, inv in zip(heads, invs):
        uw = _dot(inv, h["rhs"])
        h["u_mat"] = uw[:, :HEAD_DIM]
        h["w_mat"] = uw[:, HEAD_DIM:]
    return heads


def _gated_out(o, z, onorm):
    ms = jnp.mean(o * o, axis=-1, keepdims=True)
    return ((o * lax.rsqrt(ms + EPS)) * onorm) * (z * _sigmoid(z))


def _gdn_prompt_kernel(q_ref, k_ref, v_ref, z_ref, g_ref, on_ref, o_ref, s_ref, *, chunk, pairs):
    grp = pl.program_id(2)

    @pl.when(grp == 0)
    def _():
        s_ref[...] = jnp.zeros(s_ref.shape, F32)

    rows = q_ref.shape[0]
    masks = _block_masks(rows, chunk)
    heads = _gdn_chunk_parts(q_ref, k_ref, v_ref, [g_ref[pp] for pp in range(pairs)], masks)
    nh = len(heads)
    wbs = [h["w_mat"].astype(BF16) for h in heads]
    qdbs = [h["q_dec"].astype(BF16) for h in heads]
    kdbs = [h["k_dec"].astype(BF16) for h in heads]
    states = [s_ref[hv] for hv in range(nh)]
    v_new = [[] for _ in range(nh)]
    q_state = [[] for _ in range(nh)]
    for n in range(rows // chunk):
        rs = slice(n * chunk, (n + 1) * chunk)
        res = [_dot(jnp.concatenate([wb[rs], qdb[rs]], axis=0), st.astype(BF16))
               for wb, qdb, st in zip(wbs, qdbs, states)]
        for hv, h in enumerate(heads):
            v_n = h["u_mat"][rs] - res[hv][0:chunk]
            v_new[hv].append(v_n)
            q_state[hv].append(res[hv][chunk:2 * chunk])
            g_last = jnp.exp(h["g_tot"][n * chunk:n * chunk + 1, :])
            states[hv] = g_last * states[hv] + _dot_tn(kdbs[hv][rs], v_n.astype(BF16))
    onorm = on_ref[...]
    for hv, h in enumerate(heads):
        hcols = slice(hv * HEAD_DIM, (hv + 1) * HEAD_DIM)
        s_ref[hv] = states[hv]
        o = (jnp.concatenate(q_state[hv], axis=0)
             + _dot(h["qk_m"], jnp.concatenate(v_new[hv], axis=0).astype(BF16)))
        o_ref[:, hcols] = _gated_out(o, z_ref[:, hcols], onorm).astype(BF16)


def _gdn_prompt(qkv, qkvz, gates, o_norm, *, batch, seq_len):
    t = qkv.shape[0]
    r = GDN_ROWS
    groups = seq_len // r
    pairs = GDN_PAIRS
    hblocks = QK_HEADS // pairs
    qw, vw = pairs * HEAD_DIM, 2 * pairs * HEAD_DIM
    kern = functools.partial(_gdn_prompt_kernel, chunk=PROMPT_CHUNK, pairs=pairs)
    row = lambda b, h, g: b * groups + g
    return pl.pallas_call(
        kern,
        grid=(batch, hblocks, groups),
        in_specs=[
            pl.BlockSpec((r, qw), lambda b, h, g: (row(b, h, g), h)),
            pl.BlockSpec((r, qw), lambda b, h, g: (row(b, h, g), hblocks + h)),
            pl.BlockSpec((r, vw), lambda b, h, g: (row(b, h, g), hblocks + h)),
            pl.BlockSpec((r, vw), lambda b, h, g: (row(b, h, g), 2 * hblocks + h)),
            pl.BlockSpec((pairs, SUBLANES, r), lambda b, h, g: (h, 0, row(b, h, g))),
            pl.BlockSpec((1, HEAD_DIM), lambda b, h, g: (0, 0)),
        ],
        out_specs=[
            pl.BlockSpec((r, vw), lambda b, h, g: (row(b, h, g), h)),
            pl.BlockSpec((None, 2 * pairs, HEAD_DIM, HEAD_DIM), lambda b, h, g: (b, h, 0, 0)),
        ],
        out_shape=[
            jax.ShapeDtypeStruct((t, VAL_DIM), BF16),
            jax.ShapeDtypeStruct((batch, V_HEADS, HEAD_DIM, HEAD_DIM), F32),
        ],
        compiler_params=_params(("parallel", "parallel", "arbitrary")),
        name="gdn_prompt",
    )(qkv, qkv, qkv, qkvz, gates, o_norm)


def _gdn_sample_kernel(q_ref, k_ref, v_ref, z_ref, g_ref, on_ref, s0_ref, o_ref, s_ref,
                       lhs_scr, u_scr, kd_scr, gl_scr, vn_scr, os_scr, *, chunk):
    rows = q_ref.shape[0]
    nseq = rows // chunk
    masks = _block_masks(rows, chunk)
    heads = _gdn_chunk_parts(q_ref, k_ref, v_ref, [g_ref[...]], masks)
    for hh, h in enumerate(heads):
        lhs_scr[hh, :, 0:chunk, :] = h["w_mat"].reshape(nseq, chunk, HEAD_DIM)
        lhs_scr[hh, :, chunk:2 * chunk, :] = h["q_dec"].reshape(nseq, chunk, HEAD_DIM)
        u_scr[hh] = h["u_mat"]
        kd_scr[hh] = h["k_dec"]
        gl_scr[hh] = jnp.broadcast_to(jnp.exp(h["g_tot"]), (rows, HEAD_DIM))
    qk_masked = [h["qk_m"] for h in heads]
    onorm = on_ref[...]

    def per_sequences(step, carry):
        work = [(step * SEQ_UNROLL + lane, hh) for lane in range(SEQ_UNROLL) for hh in range(2)]
        rws = [pl.ds(pl.multiple_of(s * chunk, chunk), chunk) for s, _ in work]
        s0s = [s0_ref[s, hh] for s, hh in work]
        res = [_dot(lhs_scr[hh, s].astype(BF16), s0.astype(BF16))
               for (s, hh), s0 in zip(work, s0s)]
        v_new = [u_scr[hh, rw, :] - r[0:chunk] for (_, hh), rw, r in zip(work, rws, res)]
        upd = [_dot_tn(kd_scr[hh, rw, :], vn) for (_, hh), rw, vn in zip(work, rws, v_new)]
        for (s, hh), rw, s0, r, vn, up in zip(work, rws, s0s, res, v_new, upd):
            vn_scr[hh, rw, :] = vn
            os_scr[hh, rw, :] = r[chunk:2 * chunk]
            s_ref[s, hh] = gl_scr[hh, rw, :][0:1, :] * s0 + up
        return carry

    lax.fori_loop(0, nseq // SEQ_UNROLL, per_sequences, 0)
    for hh in range(2):
        hcols = slice(hh * HEAD_DIM, (hh + 1) * HEAD_DIM)
        o = os_scr[hh] + _dot(qk_masked[hh], vn_scr[hh].astype(BF16))
        o_ref[:, hcols] = _gated_out(o, z_ref[:, hcols], onorm).astype(BF16)


def _gdn_sample(qkv, qkvz, gates, o_norm, s0, *, chunk):
    t = qkv.shape[0]
    r = GDN_ROWS
    nseq = r // chunk
    kern = functools.partial(_gdn_sample_kernel, chunk=chunk)
    return pl.pallas_call(
        kern,
        grid=(t // r, QK_HEADS),
        in_specs=[
            pl.BlockSpec((r, HEAD_DIM), lambda i, h: (i, h)),
            pl.BlockSpec((r, HEAD_DIM), lambda i, h: (i, QK_HEADS + h)),
            pl.BlockSpec((r, 2 * HEAD_DIM), lambda i, h: (i, QK_HEADS + h)),
            pl.BlockSpec((r, 2 * HEAD_DIM), lambda i, h: (i, 2 * QK_HEADS + h)),
            pl.BlockSpec((None, SUBLANES, r), lambda i, h: (h, 0, i)),
            pl.BlockSpec((1, HEAD_DIM), lambda i, h: (0, 0)),
            pl.BlockSpec((nseq, 2, HEAD_DIM, HEAD_DIM), lambda i, h: (i, h, 0, 0)),
        ],
        out_specs=[
            pl.BlockSpec((r, 2 * HEAD_DIM), lambda i, h: (i, h)),
            pl.BlockSpec((nseq, 2, HEAD_DIM, HEAD_DIM), lambda i, h: (i, h, 0, 0)),
        ],
        out_shape=[
            jax.ShapeDtypeStruct((t, VAL_DIM), BF16),
            jax.ShapeDtypeStruct(s0.shape, F32),
        ],
        scratch_shapes=[
            pltpu.VMEM((2, nseq, 2 * chunk, HEAD_DIM), F32),
            pltpu.VMEM((2, r, HEAD_DIM), F32),
            pltpu.VMEM((2, r, HEAD_DIM), F32),
            pltpu.VMEM((2, r, HEAD_DIM), F32),
            pltpu.VMEM((2, r, HEAD_DIM), F32),
            pltpu.VMEM((2, r, HEAD_DIM), F32),
        ],
        compiler_params=_params(("parallel", "parallel")),
        name="gdn_sample",
    )(qkv, qkv, qkv, qkvz, gates, o_norm, s0)


def _front_pad_rows(state, rows):
    b, w, c = state.shape
    return jnp.pad(state, ((0, 0), (0, rows - w), (0, 0))).reshape(b * rows, c)


def _trunk(x, conv_a, conv_b, ssm_b, wts, *, grouped):
    batch, seq_len, _ = x.shape
    t = batch * seq_len
    x0 = x.reshape(t, D_MODEL)
    if grouped:
        state_a = _front_pad_rows(conv_a, SUBLANES)
        state_b = _front_pad_rows(conv_b, SUBLANES)
        chunk = seq_len
    else:
        state_a = jnp.zeros((SUBLANES, D_MODEL), F32)
        state_b = None
        chunk = PROMPT_CHUNK

    u, y = _mixer_a(x0, wts["norm_a"], wts["w_in_a"], wts["w_conv_a"], state_a,
                    seq_len=seq_len, grouped=grouped)
    new_conv_a = u.reshape(batch, seq_len, D_MODEL)[:, seq_len - (CONV_A_WIDTH - 1):]
    x1 = _matmul_residual(y, wts["w_out_a"], x0)
    x2 = _mlp(x1, wts["mlp_norm"], wts["w_up"], wts["w_down"], wts["final_norm"], layer=0,
              final_norm=False)

    qkvz, gates = _gdn_in(x2, wts["norm_b"], wts["w_in_b_t"], wts["a_log"], wts["dt_bias"],
                          chunk=chunk)
    new_conv_b = qkvz.reshape(batch, seq_len, QKVZ_DIM)[:, seq_len - (CONV_B_WIDTH - 1):, :CONV_DIM]
    qkv = _gdn_conv(qkvz, state_b, wts["w_conv_b"], seq_len=seq_len, grouped=grouped)
    if grouped:
        o, s_new = _gdn_sample(qkv, qkvz, gates, wts["o_norm"], ssm_b, chunk=chunk)
    else:
        o, s_new = _gdn_prompt(qkv, qkvz, gates, wts["o_norm"], batch=batch, seq_len=seq_len)
    x3 = _matmul_residual(o, wts["w_out_b"], x2)
    x4 = _mlp(x3, wts["mlp_norm"], wts["w_up"], wts["w_down"], wts["final_norm"], layer=1,
              final_norm=True)
    return (x4.reshape(batch, seq_len, D_MODEL), new_conv_a[None], new_conv_b[None], s_new[None])


def kernel(x_prompt, x_sample, state_conv_a, state_conv_b, state_ssm_b, norm_a, w_in_a, w_conv_a,
           w_out_a, norm_b, w_in_b, w_conv_b, a_log_b, dt_bias_b, o_norm_b, w_out_b, mlp_norm,
           w_up, w_down, final_norm):
    wts = {
        "norm_a": norm_a[0][None],
        "w_in_a": w_in_a[0].astype(BF16),
        "w_conv_a": w_conv_a[0],
        "w_out_a": w_out_a[0].astype(BF16),
        "norm_b": norm_b[0][None],
        "w_in_b_t": w_in_b[0].T.astype(BF16),
        "w_conv_b": w_conv_b[0],
        "a_log": a_log_b[0][:, None],
        "dt_bias": dt_bias_b[0][:, None],
        "o_norm": o_norm_b[0][None],
        "w_out_b": w_out_b[0].astype(BF16),
        "mlp_norm": mlp_norm[:, None, :],
        "w_up": w_up.astype(BF16),
        "w_down": w_down.astype(BF16),
        "final_norm": final_norm[None],
    }
    y_p, ca_p, cb_p, s_p = _trunk(x_prompt, None, None, None, wts, grouped=False)
    y_s, ca_s, cb_s, s_s = _trunk(x_sample, state_conv_a[0], state_conv_b[0], state_ssm_b[0], wts,
                                  grouped=True)
    return (y_p, y_s, ca_p, cb_p, s_p, ca_s, cb_s, s_s)
```

```python
import functools

import jax
import jax.numpy as jnp
from jax import lax
from jax.experimental import pallas as pl
from jax.experimental.pallas import tpu as pltpu

D_MODEL = 2048
D_FF = 4 * D_MODEL
HEAD_DIM = 128
QK_HEADS = D_MODEL // HEAD_DIM
V_HEADS = 2 * QK_HEADS
KEY_DIM = QK_HEADS * HEAD_DIM
VAL_DIM = V_HEADS * HEAD_DIM
CONV_DIM = 2 * KEY_DIM + VAL_DIM
QKVZ_DIM = CONV_DIM + VAL_DIM
CONV_A_WIDTH = 3
CONV_B_WIDTH = 4
PROMPT_CHUNK = 64
EPS = 1e-6

F32 = jnp.float32
BF16 = jnp.bfloat16

SUBLANES = 8
MXU_COLS = 256
ROW_TILE = 1024
CONV_TILE = 1024
CONV_STRIP = 32
GDN_ROWS = 256
GDN_PAIRS = 8
SEQ_UNROLL = 16
VMEM_LIMIT = 56 * 1024 * 1024


def _params(semantics):
    return pltpu.CompilerParams(dimension_semantics=semantics, vmem_limit_bytes=VMEM_LIMIT)


def _rmsnorm(x, w):
    ms = jnp.mean(x * x, axis=-1, keepdims=True)
    return (x * lax.rsqrt(ms + EPS)) * w


def _dot(a, b):
    return jnp.dot(a, b, preferred_element_type=F32)


def _dot_nt(a, b):
    return lax.dot_general(a, b, (((1,), (1,)), ((), ())), preferred_element_type=F32)


def _dot_tn(a, b):
    return lax.dot_general(a, b, (((0,), (0,)), ((), ())), preferred_element_type=F32)


def _sigmoid(x):
    return 1.0 / (1.0 + jnp.exp(-x))


def _softplus(x):
    return jnp.maximum(x, 0.0) + jnp.log1p(jnp.exp(-jnp.abs(x)))


def _prev_rows_seq(u, halo, j):
    tm = u.shape[0]
    return pltpu.roll(jnp.concatenate([u, halo], axis=0), j, 0)[0:tm]


def _state_rows_grouped(shifted, state, j, width):
    tm = shifted.shape[0]
    rmod = lax.broadcasted_iota(jnp.int32, shifted.shape, 0) & (SUBLANES - 1)
    back = width - 1 - j
    st = state if back == 0 else pltpu.roll(state, tm - back, 0)
    return jnp.where(rmod < j, st, shifted)


def _prev_rows_grouped(u, state, j, width):
    return _state_rows_grouped(pltpu.roll(u, j, 0), state, j, width)


def _mixer_a_kernel(x_ref, nw_ref, wb_ref, wc_ref, wx_ref, cw_ref, st_ref, u_ref, y_ref,
                    hs_ref, carry_ref, *, tiles_per_seq, grouped):
    i = pl.program_id(0)
    j = pl.program_id(1)

    @pl.when(j == 0)
    def _():
        hs_ref[...] = _rmsnorm(x_ref[...], nw_ref[...]).astype(BF16)

    hs = hs_ref[...]
    tm = hs.shape[0]
    for c in range(wb_ref.shape[1] // MXU_COLS):
        cols = slice(c * MXU_COLS, (c + 1) * MXU_COLS)
        gate_b = _dot(hs, wb_ref[:, cols])
        u = _dot(hs, wc_ref[:, cols]) * _dot(hs, wx_ref[:, cols])
        u_ref[:, cols] = u
        if grouped:
            state = st_ref[:, cols]
            prev1 = _prev_rows_grouped(u, state, 1, CONV_A_WIDTH)
            prev2 = _prev_rows_grouped(u, state, 2, CONV_A_WIDTH)
        else:
            halo = jnp.where(i % tiles_per_seq != 0, carry_ref[j, :, cols], 0.0)
            prev1 = _prev_rows_seq(u, halo, 1)
            prev2 = _prev_rows_seq(u, halo, 2)
            carry_ref[j, :, cols] = u[tm - SUBLANES:tm]
        y = prev2 * cw_ref[0:1, cols] + prev1 * cw_ref[1:2, cols] + u * cw_ref[2:3, cols]
        y_ref[:, cols] = (gate_b * y).astype(BF16)


def _mixer_a(x, norm_w, w_in, conv_w, state, *, seq_len, grouped):
    t = x.shape[0]
    tm, tn = ROW_TILE, 512
    nj = D_MODEL // tn
    if grouped:
        st_spec = pl.BlockSpec((tm, tn), lambda i, j: (i, j))
    else:
        st_spec = pl.BlockSpec((SUBLANES, tn), lambda i, j: (0, j))
    kern = functools.partial(_mixer_a_kernel, tiles_per_seq=max(seq_len // tm, 1), grouped=grouped)
    return pl.pallas_call(
        kern,
        grid=(t // tm, nj),
        in_specs=[
            pl.BlockSpec((tm, D_MODEL), lambda i, j: (i, 0)),
            pl.BlockSpec((1, D_MODEL), lambda i, j: (0, 0)),
            pl.BlockSpec((D_MODEL, tn), lambda i, j: (0, j)),
            pl.BlockSpec((D_MODEL, tn), lambda i, j: (0, nj + j)),
            pl.BlockSpec((D_MODEL, tn), lambda i, j: (0, 2 * nj + j)),
            pl.BlockSpec((CONV_A_WIDTH, tn), lambda i, j: (0, j)),
            st_spec,
        ],
        out_specs=[
            pl.BlockSpec((tm, tn), lambda i, j: (i, j)),
            pl.BlockSpec((tm, tn), lambda i, j: (i, j)),
        ],
        out_shape=[
            jax.ShapeDtypeStruct((t, D_MODEL), F32),
            jax.ShapeDtypeStruct((t, D_MODEL), BF16),
        ],
        scratch_shapes=[
            pltpu.VMEM((tm, D_MODEL), BF16),
            pltpu.VMEM((nj, SUBLANES, tn), F32),
        ],
        compiler_params=_params(("arbitrary", "arbitrary")),
        name="mixer_a",
    )(x, norm_w, w_in, w_in, w_in, conv_w, state)


def _matmul_residual_kernel(a_ref, w_ref, r_ref, o_ref):
    o_ref[...] = r_ref[...] + _dot(a_ref[...], w_ref[...])


def _matmul_residual(a, w, res):
    t, k = a.shape
    n = w.shape[1]
    tm, tn = ROW_TILE, 1024
    return pl.pallas_call(
        _matmul_residual_kernel,
        grid=(t // tm, n // tn),
        in_specs=[
            pl.BlockSpec((tm, k), lambda i, j: (i, 0)),
            pl.BlockSpec((k, tn), lambda i, j: (0, j)),
            pl.BlockSpec((tm, tn), lambda i, j: (i, j)),
        ],
        out_specs=pl.BlockSpec((tm, tn), lambda i, j: (i, j)),
        out_shape=jax.ShapeDtypeStruct((t, n), F32),
        compiler_params=_params(("parallel", "arbitrary")),
        name="matmul_residual",
    )(a, w, res)


def _mlp_kernel(x_ref, nw_ref, wu_ref, wd_ref, fw_ref, o_ref, hs_ref, *, final_norm):
    j = pl.program_id(1)

    @pl.when(j == 0)
    def _():
        x = x_ref[...]
        hs_ref[...] = _rmsnorm(x, nw_ref[...]).astype(BF16)
        o_ref[...] = x

    a = _dot(hs_ref[...], wu_ref[...])
    a = jnp.square(jnp.maximum(a, 0.0)).astype(BF16)
    o_ref[...] += _dot(a, wd_ref[...])

    if final_norm:
        @pl.when(j == pl.num_programs(1) - 1)
        def _():
            o_ref[...] = _rmsnorm(o_ref[...], fw_ref[...])


def _mlp(x, norm_w, w_up, w_down, final_w, *, layer, final_norm):
    t = x.shape[0]
    tm, tf = ROW_TILE, 512
    kern = functools.partial(_mlp_kernel, final_norm=final_norm)
    return pl.pallas_call(
        kern,
        grid=(t // tm, D_FF // tf),
        in_specs=[
            pl.BlockSpec((tm, D_MODEL), lambda i, j: (i, 0)),
            pl.BlockSpec((None, 1, D_MODEL), lambda i, j: (layer, 0, 0)),
            pl.BlockSpec((None, D_MODEL, tf), lambda i, j: (layer, 0, j)),
            pl.BlockSpec((None, tf, D_MODEL), lambda i, j: (layer, j, 0)),
            pl.BlockSpec((1, D_MODEL), lambda i, j: (0, 0)),
        ],
        out_specs=pl.BlockSpec((tm, D_MODEL), lambda i, j: (i, 0)),
        out_shape=jax.ShapeDtypeStruct((t, D_MODEL), F32),
        scratch_shapes=[pltpu.VMEM((tm, D_MODEL), BF16)],
        compiler_params=_params(("parallel", "arbitrary")),
        name="mlp",
    )(x, norm_w, w_up, w_down, final_w)


def _gdn_in_kernel(x_ref, nw_ref, w_ref, wg_ref, alog_ref, dtb_ref, qkvz_ref, gates_ref, hs_ref,
                   *, chunk):
    j = pl.program_id(1)

    @pl.when(j == 0)
    def _():
        hs = _rmsnorm(x_ref[...], nw_ref[...]).astype(BF16)
        hs_ref[...] = hs
        raw = _dot_nt(wg_ref[...], hs)
        tm = raw.shape[1]
        beta = _sigmoid(raw[0:V_HEADS])
        g = -jnp.exp(alog_ref[...]) * _softplus(raw[V_HEADS:2 * V_HEADS] + dtb_ref[...])
        pos = lax.broadcasted_iota(jnp.int32, g.shape, 1) & (chunk - 1)
        csum = g
        ssum = g
        s = 1
        while s < chunk:
            csum = csum + jnp.where(pos >= s, pltpu.roll(csum, s, 1), 0.0)
            ssum = ssum + jnp.where(pos < chunk - s, pltpu.roll(ssum, tm - s, 1), 0.0)
            s *= 2
        rest = ssum - g
        zero2 = jnp.zeros((2, tm), F32)
        for p in range(QK_HEADS):
            gates_ref[p, 0:2, :] = beta[2 * p:2 * p + 2]
            gates_ref[p, 2:4, :] = csum[2 * p:2 * p + 2]
            gates_ref[p, 4:6, :] = rest[2 * p:2 * p + 2]
            gates_ref[p, 6:8, :] = zero2

    qkvz_ref[...] = _dot_nt(hs_ref[...], w_ref[...])


def _gdn_in(x, norm_w, w_in_t, a_log_col, dt_bias_col, *, chunk):
    t = x.shape[0]
    tm, tn = ROW_TILE, 1024
    kern = functools.partial(_gdn_in_kernel, chunk=chunk)
    return pl.pallas_call(
        kern,
        grid=(t // tm, QKVZ_DIM // tn),
        in_specs=[
            pl.BlockSpec((tm, D_MODEL), lambda i, j: (i, 0)),
            pl.BlockSpec((1, D_MODEL), lambda i, j: (0, 0)),
            pl.BlockSpec((tn, D_MODEL), lambda i, j: (j, 0)),
            pl.BlockSpec((2 * V_HEADS, D_MODEL), lambda i, j: (QKVZ_DIM // (2 * V_HEADS), 0)),
            pl.BlockSpec((V_HEADS, 1), lambda i, j: (0, 0)),
            pl.BlockSpec((V_HEADS, 1), lambda i, j: (0, 0)),
        ],
        out_specs=[
            pl.BlockSpec((tm, tn), lambda i, j: (i, j)),
            pl.BlockSpec((QK_HEADS, SUBLANES, tm), lambda i, j: (0, 0, i)),
        ],
        out_shape=[
            jax.ShapeDtypeStruct((t, QKVZ_DIM), F32),
            jax.ShapeDtypeStruct((QK_HEADS, SUBLANES, t), F32),
        ],
        scratch_shapes=[pltpu.VMEM((tm, D_MODEL), BF16)],
        compiler_params=_params(("parallel", "arbitrary")),
        name="gdn_in",
    )(x, norm_w, w_in_t, w_in_t, a_log_col, dt_bias_col)


def _gdn_conv_kernel(x_ref, st_ref, cw_ref, o_ref, slab_ref, *, tiles_per_seq, grouped, tn):
    i = pl.program_id(0)
    j = pl.program_id(1)
    tm = x_ref.shape[0]
    n_qk = 2 * KEY_DIM // tn

    def run(normalise):
        scale = jnp.where(j < KEY_DIM // tn, HEAD_DIM ** -0.5, 1.0).astype(F32)
        for hd in range(tn // HEAD_DIM):
            cols = slice(hd * HEAD_DIM, (hd + 1) * HEAD_DIM)
            taps = [cw_ref[s:s + 1, cols] for s in range(CONV_B_WIDTH)]
            if grouped:
                slab_ref[0:SUBLANES, :] = jnp.zeros((SUBLANES, HEAD_DIM), F32)
            else:
                slab_ref[0:SUBLANES, :] = jnp.where(i % tiles_per_seq != 0, st_ref[:, cols], 0.0)
            slab_ref[SUBLANES:SUBLANES + tm, :] = x_ref[:, cols]

            def strip(rb, carry):
                r0 = pl.multiple_of(rb * CONV_STRIP, CONV_STRIP)
                u = slab_ref[pl.ds(r0 + SUBLANES, CONV_STRIP), :]
                prev = [slab_ref[pl.ds(r0 + SUBLANES - s, CONV_STRIP), :] for s in (1, 2, 3)]
                if grouped:
                    state = st_ref[pl.ds(r0, CONV_STRIP), cols]
                    prev = [_state_rows_grouped(prev[s - 1], state, s, CONV_B_WIDTH) for s in (1, 2, 3)]
                y = prev[2] * taps[0] + prev[1] * taps[1] + prev[0] * taps[2] + u * taps[3]
                o_ref[pl.ds(r0, CONV_STRIP), cols] = y * _sigmoid(y)
                return carry

            lax.fori_loop(0, tm // CONV_STRIP, strip, 0, unroll=True)
            if normalise:
                y = o_ref[:, cols]
                ss = jnp.sum(y * y, axis=-1, keepdims=True)
                o_ref[:, cols] = (y * lax.rsqrt(ss + EPS)) * scale

    @pl.when(j >= n_qk)
    def _():
        run(False)

    @pl.when(j < n_qk)
    def _():
        run(True)


def _gdn_conv(qkvz, state, conv_w, *, seq_len, grouped):
    t = qkvz.shape[0]
    tm, tn = CONV_TILE, 1024
    if grouped:
        st_spec = pl.BlockSpec((tm, tn), lambda i, j: (i, j))
        st = state
    else:
        blocks_per_tile = tm // SUBLANES
        st_spec = pl.BlockSpec((SUBLANES, tn), lambda i, j: (jnp.maximum(i * blocks_per_tile - 1, 0), j))
        st = qkvz
    kern = functools.partial(_gdn_conv_kernel, tiles_per_seq=max(seq_len // tm, 1), grouped=grouped, tn=tn)
    return pl.pallas_call(
        kern,
        grid=(t // tm, CONV_DIM // tn),
        in_specs=[
            pl.BlockSpec((tm, tn), lambda i, j: (i, j)),
            st_spec,
            pl.BlockSpec((CONV_B_WIDTH, tn), lambda i, j: (0, j)),
        ],
        out_specs=pl.BlockSpec((tm, tn), lambda i, j: (i, j)),
        out_shape=jax.ShapeDtypeStruct((t, CONV_DIM), F32),
        scratch_shapes=[pltpu.VMEM((SUBLANES + tm, HEAD_DIM), F32)],
        compiler_params=_params(("parallel", "parallel")),
        name="gdn_conv",
    )(qkvz, st, conv_w)


def _block_masks(rows, chunk):
    levels = chunk.bit_length() - 1
    ri = lax.broadcasted_iota(jnp.int32, (rows, rows), 0)
    ci = lax.broadcasted_iota(jnp.int32, (rows, rows), 1)
    same = [None] + [lax.shift_right_logical(ri, lb) == lax.shift_right_logical(ci, lb)
                     for lb in range(1, levels + 1)]
    return {
        "diag": ri == ci,
        "pair": same[1],
        "join": [same[lb + 1] & ~same[lb] for lb in range(1, levels)],
        "low": same[levels] & (ri >= ci),
        "strict": same[levels] & (ri > ci),
    }


def _gate_cols(g):
    rows = g.shape[1]
    padded = jnp.concatenate([g, jnp.zeros((HEAD_DIM - SUBLANES, rows), F32)], axis=0)
    return padded.T


def _gdn_chunk_parts(q_ref, k_ref, v_ref, gates, masks):
    heads = []
    for pp, grow in enumerate(gates):
        qcols = slice(pp * HEAD_DIM, (pp + 1) * HEAD_DIM)
        q = q_ref[:, qcols]
        k = k_ref[:, qcols]
        kb = k.astype(BF16)
        kk = _dot_nt(kb, kb)
        qk = _dot_nt(q.astype(BF16), kb)
        cols = _gate_cols(grow)
        for hh in range(2):
            hv = 2 * pp + hh
            beta_c = cols[:, hh:hh + 1]
            g_c = cols[:, 2 + hh:3 + hh]
            x_c = cols[:, 4 + hh:5 + hh]
            g_r = grow[2 + hh:3 + hh, :]
            decay = jnp.exp(jnp.where(masks["low"], g_c - g_r, -jnp.inf))
            eg = jnp.exp(g_c)
            v = v_ref[:, hv * HEAD_DIM:(hv + 1) * HEAD_DIM]
            heads.append({
                "l_mat": jnp.where(masks["strict"], beta_c * kk * decay, 0.0).astype(BF16),
                "rhs": jnp.concatenate([v * beta_c, k * (beta_c * eg)], axis=1).astype(BF16),
                "qk_m": (qk * decay).astype(BF16),
                "q_dec": q * eg,
                "k_dec": k * jnp.exp(x_c),
                "g_tot": g_c + x_c,
            })
    invs = [jnp.where(masks["diag"], 1.0, jnp.where(masks["pair"], -h["l_mat"], 0.0)) for h in heads]
    for join in masks["join"]:
        cross = [_dot(jnp.where(join, h["l_mat"], 0.0), inv).astype(BF16)
                 for h, inv in zip(heads, invs)]
        invs = [inv - _dot(inv, cr).astype(BF16) for inv, cr in zip(invs, cross)]
    for h, inv in zip(heads, invs):
        uw = _dot(inv, h["rhs"])
        h["u_mat"] = uw[:, :HEAD_DIM]
        h["w_mat"] = uw[:, HEAD_DIM:]
    return heads


def _gated_out(o, z, onorm):
    ms = jnp.mean(o * o, axis=-1, keepdims=True)
    return ((o * lax.rsqrt(ms + EPS)) * onorm) * (z * _sigmoid(z))


def _gdn_prompt_kernel(q_ref, k_ref, v_ref, z_ref, g_ref, on_ref, o_ref, s_ref, *, chunk, pairs):
    grp = pl.program_id(2)

    @pl.when(grp == 0)
    def _():
        s_ref[...] = jnp.zeros(s_ref.shape, F32)

    rows = q_ref.shape[0]
    masks = _block_masks(rows, chunk)
    heads = _gdn_chunk_parts(q_ref, k_ref, v_ref, [g_ref[pp] for pp in range(pairs)], masks)
    nh = len(heads)
    wbs = [h["w_mat"].astype(BF16) for h in heads]
    qdbs = [h["q_dec"].astype(BF16) for h in heads]
    kdbs = [h["k_dec"].astype(BF16) for h in heads]
    states = [s_ref[hv] for hv in range(nh)]
    v_new = [[] for _ in range(nh)]
    q_state = [[] for _ in range(nh)]
    for n in range(rows // chunk):
        rs = slice(n * chunk, (n + 1) * chunk)
        res = [_dot(jnp.concatenate([wb[rs], qdb[rs]], axis=0), st.astype(BF16))
               for wb, qdb, st in zip(wbs, qdbs, states)]
        for hv, h in enumerate(heads):
            v_n = h["u_mat"][rs] - res[hv][0:chunk]
            v_new[hv].append(v_n)
            q_state[hv].append(res[hv][chunk:2 * chunk])
            g_last = jnp.exp(h["g_tot"][n * chunk:n * chunk + 1, :])
            states[hv] = g_last * states[hv] + _dot_tn(kdbs[hv][rs], v_n.astype(BF16))
    onorm = on_ref[...]
    for hv, h in enumerate(heads):
        hcols = slice(hv * HEAD_DIM, (hv + 1) * HEAD_DIM)
        s_ref[hv] = states[hv]
        o = (jnp.concatenate(q_state[hv], axis=0)
             + _dot(h["qk_m"], jnp.concatenate(v_new[hv], axis=0).astype(BF16)))
        o_ref[:, hcols] = _gated_out(o, z_ref[:, hcols], onorm).astype(BF16)


def _gdn_prompt(qkv, qkvz, gates, o_norm, *, batch, seq_len):
    t = qkv.shape[0]
    r = GDN_ROWS
    groups = seq_len // r
    pairs = GDN_PAIRS
    hblocks = QK_HEADS // pairs
    qw, vw = pairs * HEAD_DIM, 2 * pairs * HEAD_DIM
    kern = functools.partial(_gdn_prompt_kernel, chunk=PROMPT_CHUNK, pairs=pairs)
    row = lambda b, h, g: b * groups + g
    return pl.pallas_call(
        kern,
        grid=(batch, hblocks, groups),
        in_specs=[
            pl.BlockSpec((r, qw), lambda b, h, g: (row(b, h, g), h)),
            pl.BlockSpec((r, qw), lambda b, h, g: (row(b, h, g), hblocks + h)),
            pl.BlockSpec((r, vw), lambda b, h, g: (row(b, h, g), hblocks + h)),
            pl.BlockSpec((r, vw), lambda b, h, g: (row(b, h, g), 2 * hblocks + h)),
            pl.BlockSpec((pairs, SUBLANES, r), lambda b, h, g: (h, 0, row(b, h, g))),
            pl.BlockSpec((1, HEAD_DIM), lambda b, h, g: (0, 0)),
        ],
        out_specs=[
            pl.BlockSpec((r, vw), lambda b, h, g: (row(b, h, g), h)),
            pl.BlockSpec((None, 2 * pairs, HEAD_DIM, HEAD_DIM), lambda b, h, g: (b, h, 0, 0)),
        ],
        out_shape=[
            jax.ShapeDtypeStruct((t, VAL_DIM), BF16),
            jax.ShapeDtypeStruct((batch, V_HEADS, HEAD_DIM, HEAD_DIM), F32),
        ],
        compiler_params=_params(("parallel", "parallel", "arbitrary")),
        name="gdn_prompt",
    )(qkv, qkv, qkv, qkvz, gates, o_norm)


def _gdn_sample_kernel(q_ref, k_ref, v_ref, z_ref, g_ref, on_ref, s0_ref, o_ref, s_ref,
                       lhs_scr, u_scr, kd_scr, gl_scr, vn_scr, os_scr, *, chunk):
    rows = q_ref.shape[0]
    nseq = rows // chunk
    masks = _block_masks(rows, chunk)
    heads = _gdn_chunk_parts(q_ref, k_ref, v_ref, [g_ref[...]], masks)
    for hh, h in enumerate(heads):
        lhs_scr[hh, :, 0:chunk, :] = h["w_mat"].reshape(nseq, chunk, HEAD_DIM)
        lhs_scr[hh, :, chunk:2 * chunk, :] = h["q_dec"].reshape(nseq, chunk, HEAD_DIM)
        u_scr[hh] = h["u_mat"]
        kd_scr[hh] = h["k_dec"]
        gl_scr[hh] = jnp.broadcast_to(jnp.exp(h["g_tot"]), (rows, HEAD_DIM))
    qk_masked = [h["qk_m"] for h in heads]
    onorm = on_ref[...]

    def per_sequences(step, carry):
        work = [(step * SEQ_UNROLL + lane, hh) for lane in range(SEQ_UNROLL) for hh in range(2)]
        rws = [pl.ds(pl.multiple_of(s * chunk, chunk), chunk) for s, _ in work]
        s0s = [s0_ref[s, hh] for s, hh in work]
        res = [_dot(lhs_scr[hh, s].astype(BF16), s0.astype(BF16))
               for (s, hh), s0 in zip(work, s0s)]
        v_new = [u_scr[hh, rw, :] - r[0:chunk] for (_, hh), rw, r in zip(work, rws, res)]
        upd = [_dot_tn(kd_scr[hh, rw, :], vn) for (_, hh), rw, vn in zip(work, rws, v_new)]
        for (s, hh), rw, s0, r, vn, up in zip(work, rws, s0s, res, v_new, upd):
            vn_scr[hh, rw, :] = vn
            os_scr[hh, rw, :] = r[chunk:2 * chunk]
            s_ref[s, hh] = gl_scr[hh, rw, :][0:1, :] * s0 + up
        return carry

    lax.fori_loop(0, nseq // SEQ_UNROLL, per_sequences, 0)
    for hh in range(2):
        hcols = slice(hh * HEAD_DIM, (hh + 1) * HEAD_DIM)
        o = os_scr[hh] + _dot(qk_masked[hh], vn_scr[hh].astype(BF16))
        o_ref[:, hcols] = _gated_out(o, z_ref[:, hcols], onorm).astype(BF16)


def _gdn_sample(qkv, qkvz, gates, o_norm, s0, *, chunk):
    t = qkv.shape[0]
    r = GDN_ROWS
    nseq = r // chunk
    kern = functools.partial(_gdn_sample_kernel, chunk=chunk)
    return pl.pallas_call(
        kern,
        grid=(t // r, QK_HEADS),
        in_specs=[
            pl.BlockSpec((r, HEAD_DIM), lambda i, h: (i, h)),
            pl.BlockSpec((r, HEAD_DIM), lambda i, h: (i, QK_HEADS + h)),
            pl.BlockSpec((r, 2 * HEAD_DIM), lambda i, h: (i, QK_HEADS + h)),
            pl.BlockSpec((r, 2 * HEAD_DIM), lambda i, h: (i, 2 * QK_HEADS + h)),
            pl.BlockSpec((None, SUBLANES, r), lambda i, h: (h, 0, i)),
            pl.BlockSpec((1, HEAD_DIM), lambda i, h: (0, 0)),
            pl.BlockSpec((nseq, 2, HEAD_DIM, HEAD_DIM), lambda i, h: (i, h, 0, 0)),
        ],
        out_specs=[
            pl.BlockSpec((r, 2 * HEAD_DIM), lambda i, h: (i, h)),
            pl.BlockSpec((nseq, 2, HEAD_DIM, HEAD_DIM), lambda i, h: (i, h, 0, 0)),
        ],
        out_shape=[
            jax.ShapeDtypeStruct((t, VAL_DIM), BF16),
            jax.ShapeDtypeStruct(s0.shape, F32),
        ],
        scratch_shapes=[
            pltpu.VMEM((2, nseq, 2 * chunk, HEAD_DIM), F32),
            pltpu.VMEM((2, r, HEAD_DIM), F32),
            pltpu.VMEM((2, r, HEAD_DIM), F32),
            pltpu.VMEM((2, r, HEAD_DIM), F32),
            pltpu.VMEM((2, r, HEAD_DIM), F32),
            pltpu.VMEM((2, r, HEAD_DIM), F32),
        ],
        compiler_params=_params(("parallel", "parallel")),
        name="gdn_sample",
    )(qkv, qkv, qkv, qkvz, gates, o_norm, s0)


def _front_pad_rows(state, rows):
    b, w, c = state.shape
    return jnp.pad(state, ((0, 0), (0, rows - w), (0, 0))).reshape(b * rows, c)


def _trunk(x, conv_a, conv_b, ssm_b, wts, *, grouped):
    batch, seq_len, _ = x.shape
    t = batch * seq_len
    x0 = x.reshape(t, D_MODEL)
    if grouped:
        state_a = _front_pad_rows(conv_a, SUBLANES)
        state_b = _front_pad_rows(conv_b, SUBLANES)
        chunk = seq_len
    else:
        state_a = jnp.zeros((SUBLANES, D_MODEL), F32)
        state_b = None
        chunk = PROMPT_CHUNK

    u, y = _mixer_a(x0, wts["norm_a"], wts["w_in_a"], wts["w_conv_a"], state_a,
                    seq_len=seq_len, grouped=grouped)
    new_conv_a = u.reshape(batch, seq_len, D_MODEL)[:, seq_len - (CONV_A_WIDTH - 1):]
    x1 = _matmul_residual(y, wts["w_out_a"], x0)
    x2 = _mlp(x1, wts["mlp_norm"], wts["w_up"], wts["w_down"], wts["final_norm"], layer=0,
              final_norm=False)

    qkvz, gates = _gdn_in(x2, wts["norm_b"], wts["w_in_b_t"], wts["a_log"], wts["dt_bias"],
                          chunk=chunk)
    new_conv_b = qkvz.reshape(batch, seq_len, QKVZ_DIM)[:, seq_len - (CONV_B_WIDTH - 1):, :CONV_DIM]
    qkv = _gdn_conv(qkvz, state_b, wts["w_conv_b"], seq_len=seq_len, grouped=grouped)
    if grouped:
        o, s_new = _gdn_sample(qkv, qkvz, gates, wts["o_norm"], ssm_b, chunk=chunk)
    else:
        o, s_new = _gdn_prompt(qkv, qkvz, gates, wts["o_norm"], batch=batch, seq_len=seq_len)
    x3 = _matmul_residual(o, wts["w_out_b"], x2)
    x4 = _mlp(x3, wts["mlp_norm"], wts["w_up"], wts["w_down"], wts["final_norm"], layer=1,
              final_norm=True)
    return (x4.reshape(batch, seq_len, D_MODEL), new_conv_a[None], new_conv_b[None], s_new[None])


def kernel(x_prompt, x_sample, state_conv_a, state_conv_b, state_ssm_b, norm_a, w_in_a, w_conv_a,
           w_out_a, norm_b, w_in_b, w_conv_b, a_log_b, dt_bias_b, o_norm_b, w_out_b, mlp_norm,
           w_up, w_down, final_norm):
    wts = {
        "norm_a": norm_a[0][None],
        "w_in_a": w_in_a[0].astype(BF16),
        "w_conv_a": w_conv_a[0],
        "w_out_a": w_out_a[0].astype(BF16),
        "norm_b": norm_b[0][None],
        "w_in_b_t": w_in_b[0].T.astype(BF16),
        "w_conv_b": w_conv_b[0],
        "a_log": a_log_b[0][:, None],
        "dt_bias": dt_bias_b[0][:, None],
        "o_norm": o_norm_b[0][None],
        "w_out_b": w_out_b[0].astype(BF16),
        "mlp_norm": mlp_norm[:, None, :],
        "w_up": w_up.astype(BF16),
        "w_down": w_down.astype(BF16),
        "final_norm": final_norm[None],
    }
    y_p, ca_p, cb_p, s_p = _trunk(x_prompt, None, None, None, wts, grouped=False)
    y_s, ca_s, cb_s, s_s = _trunk(x_sample, state_conv_a[0], state_conv_b[0], state_ssm_b[0], wts,
                                  grouped=True)
    return (y_p, y_s, ca_p, cb_p, s_p, ca_s, cb_s, s_s)
```

```python
import functools

import jax
import jax.numpy as jnp
from jax import lax
from jax.experimental import pallas as pl
from jax.experimental.pallas import tpu as pltpu

D_MODEL = 2048
D_FF = 4 * D_MODEL
HEAD_DIM = 128
QK_HEADS = D_MODEL // HEAD_DIM
V_HEADS = 2 * QK_HEADS
KEY_DIM = QK_HEADS * HEAD_DIM
VAL_DIM = V_HEADS * HEAD_DIM
CONV_DIM = 2 * KEY_DIM + VAL_DIM
QKVZ_DIM = CONV_DIM + VAL_DIM
CONV_A_WIDTH = 3
CONV_B_WIDTH = 4
PROMPT_CHUNK = 64
EPS = 1e-6

F32 = jnp.float32
BF16 = jnp.bfloat16

SUBLANES = 8
MXU_COLS = 256
ROW_TILE = 1024
CONV_TILE = 1024
CONV_STRIP = 32
PROMPT_GDN_ROWS = 128
SAMPLE_GDN_ROWS = 256
GDN_PAIRS = 8
SEQ_UNROLL = 16
VMEM_LIMIT = 56 * 1024 * 1024


def _params(semantics):
    return pltpu.CompilerParams(dimension_semantics=semantics, vmem_limit_bytes=VMEM_LIMIT)


def _rmsnorm(x, w):
    ms = jnp.mean(x * x, axis=-1, keepdims=True)
    return (x * lax.rsqrt(ms + EPS)) * w


def _dot(a, b):
    return jnp.dot(a, b, preferred_element_type=F32)


def _dot_nt(a, b):
    return lax.dot_general(a, b, (((1,), (1,)), ((), ())), preferred_element_type=F32)


def _dot_tn(a, b):
    return lax.dot_general(a, b, (((0,), (0,)), ((), ())), preferred_element_type=F32)


def _sigmoid(x):
    return 1.0 / (1.0 + jnp.exp(-x))


def _softplus(x):
    return jnp.maximum(x, 0.0) + jnp.log1p(jnp.exp(-jnp.abs(x)))


def _prev_rows_seq(u, halo, j):
    tm = u.shape[0]
    return pltpu.roll(jnp.concatenate([u, halo], axis=0), j, 0)[0:tm]


def _state_rows_grouped(shifted, state, j, width):
    tm = shifted.shape[0]
    rmod = lax.broadcasted_iota(jnp.int32, shifted.shape, 0) & (SUBLANES - 1)
    back = width - 1 - j
    st = state if back == 0 else pltpu.roll(state, tm - back, 0)
    return jnp.where(rmod < j, st, shifted)


def _prev_rows_grouped(u, state, j, width):
    return _state_rows_grouped(pltpu.roll(u, j, 0), state, j, width)


def _mixer_a_kernel(x_ref, nw_ref, wb_ref, wc_ref, wx_ref, cw_ref, st_ref, u_ref, y_ref,
                    hs_ref, carry_ref, *, tiles_per_seq, grouped):
    i = pl.program_id(0)
    j = pl.program_id(1)

    @pl.when(j == 0)
    def _():
        hs_ref[...] = _rmsnorm(x_ref[...], nw_ref[...]).astype(BF16)

    hs = hs_ref[...]
    tm = hs.shape[0]
    for c in range(wb_ref.shape[1] // MXU_COLS):
        cols = slice(c * MXU_COLS, (c + 1) * MXU_COLS)
        gate_b = _dot(hs, wb_ref[:, cols])
        u = _dot(hs, wc_ref[:, cols]) * _dot(hs, wx_ref[:, cols])
        u_ref[:, cols] = u
        if grouped:
            state = st_ref[:, cols]
            prev1 = _prev_rows_grouped(u, state, 1, CONV_A_WIDTH)
            prev2 = _prev_rows_grouped(u, state, 2, CONV_A_WIDTH)
        else:
            halo = jnp.where(i % tiles_per_seq != 0, carry_ref[j, :, cols], 0.0)
            prev1 = _prev_rows_seq(u, halo, 1)
            prev2 = _prev_rows_seq(u, halo, 2)
            carry_ref[j, :, cols] = u[tm - SUBLANES:tm]
        y = prev2 * cw_ref[0:1, cols] + prev1 * cw_ref[1:2, cols] + u * cw_ref[2:3, cols]
        y_ref[:, cols] = (gate_b * y).astype(BF16)


def _mixer_a(x, norm_w, w_in, conv_w, state, *, seq_len, grouped):
    t = x.shape[0]
    tm, tn = ROW_TILE, 512
    nj = D_MODEL // tn
    if grouped:
        st_spec = pl.BlockSpec((tm, tn), lambda i, j: (i, j))
    else:
        st_spec = pl.BlockSpec((SUBLANES, tn), lambda i, j: (0, j))
    kern = functools.partial(_mixer_a_kernel, tiles_per_seq=max(seq_len // tm, 1), grouped=grouped)
    return pl.pallas_call(
        kern,
        grid=(t // tm, nj),
        in_specs=[
            pl.BlockSpec((tm, D_MODEL), lambda i, j: (i, 0)),
            pl.BlockSpec((1, D_MODEL), lambda i, j: (0, 0)),
            pl.BlockSpec((D_MODEL, tn), lambda i, j: (0, j)),
            pl.BlockSpec((D_MODEL, tn), lambda i, j: (0, nj + j)),
            pl.BlockSpec((D_MODEL, tn), lambda i, j: (0, 2 * nj + j)),
            pl.BlockSpec((CONV_A_WIDTH, tn), lambda i, j: (0, j)),
            st_spec,
        ],
        out_specs=[
            pl.BlockSpec((tm, tn), lambda i, j: (i, j)),
            pl.BlockSpec((tm, tn), lambda i, j: (i, j)),
        ],
        out_shape=[
            jax.ShapeDtypeStruct((t, D_MODEL), F32),
            jax.ShapeDtypeStruct((t, D_MODEL), BF16),
        ],
        scratch_shapes=[
            pltpu.VMEM((tm, D_MODEL), BF16),
            pltpu.VMEM((nj, SUBLANES, tn), F32),
        ],
        compiler_params=_params(("arbitrary", "arbitrary")),
        name="mixer_a",
    )(x, norm_w, w_in, w_in, w_in, conv_w, state)


def _matmul_residual_kernel(a_ref, w_ref, r_ref, o_ref):
    o_ref[...] = r_ref[...] + _dot(a_ref[...], w_ref[...])


def _matmul_residual(a, w, res):
    t, k = a.shape
    n = w.shape[1]
    tm, tn = ROW_TILE, 1024
    return pl.pallas_call(
        _matmul_residual_kernel,
        grid=(t // tm, n // tn),
        in_specs=[
            pl.BlockSpec((tm, k), lambda i, j: (i, 0)),
            pl.BlockSpec((k, tn), lambda i, j: (0, j)),
            pl.BlockSpec((tm, tn), lambda i, j: (i, j)),
        ],
        out_specs=pl.BlockSpec((tm, tn), lambda i, j: (i, j)),
        out_shape=jax.ShapeDtypeStruct((t, n), F32),
        compiler_params=_params(("parallel", "arbitrary")),
        name="matmul_residual",
    )(a, w, res)


def _mlp_kernel(x_ref, nw_ref, wu_ref, wd_ref, fw_ref, o_ref, hs_ref, *, final_norm):
    j = pl.program_id(1)

    @pl.when(j == 0)
    def _():
        x = x_ref[...]
        hs_ref[...] = _rmsnorm(x, nw_ref[...]).astype(BF16)
        o_ref[...] = x

    a = _dot(hs_ref[...], wu_ref[...])
    a = jnp.square(jnp.maximum(a, 0.0)).astype(BF16)
    o_ref[...] += _dot(a, wd_ref[...])

    if final_norm:
        @pl.when(j == pl.num_programs(1) - 1)
        def _():
            o_ref[...] = _rmsnorm(o_ref[...], fw_ref[...])


def _mlp(x, norm_w, w_up, w_down, final_w, *, layer, final_norm):
    t = x.shape[0]
    tm, tf = ROW_TILE, 512
    kern = functools.partial(_mlp_kernel, final_norm=final_norm)
    return pl.pallas_call(
        kern,
        grid=(t // tm, D_FF // tf),
        in_specs=[
            pl.BlockSpec((tm, D_MODEL), lambda i, j: (i, 0)),
            pl.BlockSpec((None, 1, D_MODEL), lambda i, j: (layer, 0, 0)),
            pl.BlockSpec((None, D_MODEL, tf), lambda i, j: (layer, 0, j)),
            pl.BlockSpec((None, tf, D_MODEL), lambda i, j: (layer, j, 0)),
            pl.BlockSpec((1, D_MODEL), lambda i, j: (0, 0)),
        ],
        out_specs=pl.BlockSpec((tm, D_MODEL), lambda i, j: (i, 0)),
        out_shape=jax.ShapeDtypeStruct((t, D_MODEL), F32),
        scratch_shapes=[pltpu.VMEM((tm, D_MODEL), BF16)],
        compiler_params=_params(("parallel", "arbitrary")),
        name="mlp",
    )(x, norm_w, w_up, w_down, final_w)


def _gdn_in_kernel(x_ref, nw_ref, w_ref, wg_ref, alog_ref, dtb_ref, qkvz_ref, gates_ref, hs_ref,
                   *, chunk):
    j = pl.program_id(1)

    @pl.when(j == 0)
    def _():
        hs = _rmsnorm(x_ref[...], nw_ref[...]).astype(BF16)
        hs_ref[...] = hs
        raw = _dot_nt(wg_ref[...], hs)
        tm = raw.shape[1]
        beta = _sigmoid(raw[0:V_HEADS])
        g = -jnp.exp(alog_ref[...]) * _softplus(raw[V_HEADS:2 * V_HEADS] + dtb_ref[...])
        pos = lax.broadcasted_iota(jnp.int32, g.shape, 1) & (chunk - 1)
        csum = g
        ssum = g
        s = 1
        while s < chunk:
            csum = csum + jnp.where(pos >= s, pltpu.roll(csum, s, 1), 0.0)
            ssum = ssum + jnp.where(pos < chunk - s, pltpu.roll(ssum, tm - s, 1), 0.0)
            s *= 2
        rest = ssum - g
        zero2 = jnp.zeros((2, tm), F32)
        for p in range(QK_HEADS):
            gates_ref[p, 0:2, :] = beta[2 * p:2 * p + 2]
            gates_ref[p, 2:4, :] = csum[2 * p:2 * p + 2]
            gates_ref[p, 4:6, :] = rest[2 * p:2 * p + 2]
            gates_ref[p, 6:8, :] = zero2

    qkvz_ref[...] = _dot_nt(hs_ref[...], w_ref[...])


def _gdn_in(x, norm_w, w_in_t, a_log_col, dt_bias_col, *, chunk):
    t = x.shape[0]
    tm, tn = ROW_TILE, 1024
    kern = functools.partial(_gdn_in_kernel, chunk=chunk)
    return pl.pallas_call(
        kern,
        grid=(t // tm, QKVZ_DIM // tn),
        in_specs=[
            pl.BlockSpec((tm, D_MODEL), lambda i, j: (i, 0)),
            pl.BlockSpec((1, D_MODEL), lambda i, j: (0, 0)),
            pl.BlockSpec((tn, D_MODEL), lambda i, j: (j, 0)),
            pl.BlockSpec((2 * V_HEADS, D_MODEL), lambda i, j: (QKVZ_DIM // (2 * V_HEADS), 0)),
            pl.BlockSpec((V_HEADS, 1), lambda i, j: (0, 0)),
            pl.BlockSpec((V_HEADS, 1), lambda i, j: (0, 0)),
        ],
        out_specs=[
            pl.BlockSpec((tm, tn), lambda i, j: (i, j)),
            pl.BlockSpec((QK_HEADS, SUBLANES, tm), lambda i, j: (0, 0, i)),
        ],
        out_shape=[
            jax.ShapeDtypeStruct((t, QKVZ_DIM), F32),
            jax.ShapeDtypeStruct((QK_HEADS, SUBLANES, t), F32),
        ],
        scratch_shapes=[pltpu.VMEM((tm, D_MODEL), BF16)],
        compiler_params=_params(("parallel", "arbitrary")),
        name="gdn_in",
    )(x, norm_w, w_in_t, w_in_t, a_log_col, dt_bias_col)


def _gdn_conv_kernel(x_ref, st_ref, cw_ref, o_ref, slab_ref, *, tiles_per_seq, grouped, tn):
    i = pl.program_id(0)
    j = pl.program_id(1)
    tm = x_ref.shape[0]
    n_qk = 2 * KEY_DIM // tn

    def run(normalise):
        scale = jnp.where(j < KEY_DIM // tn, HEAD_DIM ** -0.5, 1.0).astype(F32)
        for hd in range(tn // HEAD_DIM):
            cols = slice(hd * HEAD_DIM, (hd + 1) * HEAD_DIM)
            taps = [cw_ref[s:s + 1, cols] for s in range(CONV_B_WIDTH)]
            if grouped:
                slab_ref[0:SUBLANES, :] = jnp.zeros((SUBLANES, HEAD_DIM), F32)
            else:
                slab_ref[0:SUBLANES, :] = jnp.where(i % tiles_per_seq != 0, st_ref[:, cols], 0.0)
            slab_ref[SUBLANES:SUBLANES + tm, :] = x_ref[:, cols]

            def strip(rb, carry):
                r0 = pl.multiple_of(rb * CONV_STRIP, CONV_STRIP)
                u = slab_ref[pl.ds(r0 + SUBLANES, CONV_STRIP), :]
                prev = [slab_ref[pl.ds(r0 + SUBLANES - s, CONV_STRIP), :] for s in (1, 2, 3)]
                if grouped:
                    state = st_ref[pl.ds(r0, CONV_STRIP), cols]
                    prev = [_state_rows_grouped(prev[s - 1], state, s, CONV_B_WIDTH) for s in (1, 2, 3)]
                y = prev[2] * taps[0] + prev[1] * taps[1] + prev[0] * taps[2] + u * taps[3]
                o_ref[pl.ds(r0, CONV_STRIP), cols] = y * _sigmoid(y)
                return carry

            lax.fori_loop(0, tm // CONV_STRIP, strip, 0, unroll=True)
            if normalise:
                y = o_ref[:, cols]
                ss = jnp.sum(y * y, axis=-1, keepdims=True)
                o_ref[:, cols] = (y * lax.rsqrt(ss + EPS)) * scale

    @pl.when(j >= n_qk)
    def _():
        run(False)

    @pl.when(j < n_qk)
    def _():
        run(True)


def _gdn_conv(qkvz, state, conv_w, *, seq_len, grouped):
    t = qkvz.shape[0]
    tm, tn = CONV_TILE, 1024
    if grouped:
        st_spec = pl.BlockSpec((tm, tn), lambda i, j: (i, j))
        st = state
    else:
        blocks_per_tile = tm // SUBLANES
        st_spec = pl.BlockSpec((SUBLANES, tn), lambda i, j: (jnp.maximum(i * blocks_per_tile - 1, 0), j))
        st = qkvz
    kern = functools.partial(_gdn_conv_kernel, tiles_per_seq=max(seq_len // tm, 1), grouped=grouped, tn=tn)
    return pl.pallas_call(
        kern,
        grid=(t // tm, CONV_DIM // tn),
        in_specs=[
            pl.BlockSpec((tm, tn), lambda i, j: (i, j)),
            st_spec,
            pl.BlockSpec((CONV_B_WIDTH, tn), lambda i, j: (0, j)),
        ],
        out_specs=pl.BlockSpec((tm, tn), lambda i, j: (i, j)),
        out_shape=jax.ShapeDtypeStruct((t, CONV_DIM), F32),
        scratch_shapes=[pltpu.VMEM((SUBLANES + tm, HEAD_DIM), F32)],
        compiler_params=_params(("parallel", "parallel")),
        name="gdn_conv",
    )(qkvz, st, conv_w)


def _block_masks(rows, chunk):
    levels = chunk.bit_length() - 1
    ri = lax.broadcasted_iota(jnp.int32, (rows, rows), 0)
    ci = lax.broadcasted_iota(jnp.int32, (rows, rows), 1)
    same = [None] + [lax.shift_right_logical(ri, lb) == lax.shift_right_logical(ci, lb)
                     for lb in range(1, levels + 1)]
    return {
        "diag": ri == ci,
        "pair": same[1],
        "join": [same[lb + 1] & ~same[lb] for lb in range(1, levels)],
        "low": same[levels] & (ri >= ci),
        "strict": same[levels] & (ri > ci),
    }


def _gate_cols(g):
    rows = g.shape[1]
    padded = jnp.concatenate([g, jnp.zeros((HEAD_DIM - SUBLANES, rows), F32)], axis=0)
    return padded.T


def _gdn_chunk_parts(q_ref, k_ref, v_ref, gates, masks):
    heads = []
    for pp, grow in enumerate(gates):
        qcols = slice(pp * HEAD_DIM, (pp + 1) * HEAD_DIM)
        q = q_ref[:, qcols]
        k = k_ref[:, qcols]
        kb = k.astype(BF16)
        kk = _dot_nt(kb, kb)
        qk = _dot_nt(q.astype(BF16), kb)
        cols = _gate_cols(grow)
        for hh in range(2):
            hv = 2 * pp + hh
            beta_c = cols[:, hh:hh + 1]
            g_c = cols[:, 2 + hh:3 + hh]
            x_c = cols[:, 4 + hh:5 + hh]
            g_r = grow[2 + hh:3 + hh, :]
            decay = jnp.exp(jnp.where(masks["low"], g_c - g_r, -jnp.inf))
            eg = jnp.exp(g_c)
            v = v_ref[:, hv * HEAD_DIM:(hv + 1) * HEAD_DIM]
            heads.append({
                "l_mat": jnp.where(masks["strict"], beta_c * kk * decay, 0.0).astype(BF16),
                "rhs": jnp.concatenate([v * beta_c, k * (beta_c * eg)], axis=1).astype(BF16),
                "qk_m": (qk * decay).astype(BF16),
                "q_dec": q * eg,
                "k_dec": k * jnp.exp(x_c),
                "g_tot": g_c + x_c,
            })
    invs = [jnp.where(masks["diag"], 1.0, jnp.where(masks["pair"], -h["l_mat"], 0.0)) for h in heads]
    for join in masks["join"]:
        cross = [_dot(jnp.where(join, h["l_mat"], 0.0), inv).astype(BF16)
                 for h, inv in zip(heads, invs)]
        invs = [inv - _dot(inv, cr).astype(BF16) for inv, cr in zip(invs, cross)]
    for h, inv in zip(heads, invs):
        uw = _dot(inv, h["rhs"])
        h["u_mat"] = uw[:, :HEAD_DIM]
        h["w_mat"] = uw[:, HEAD_DIM:]
    return heads


def _gated_out(o, z, onorm):
    ms = jnp.mean(o * o, axis=-1, keepdims=True)
    return ((o * lax.rsqrt(ms + EPS)) * onorm) * (z * _sigmoid(z))


def _gdn_prompt_kernel(q_ref, k_ref, v_ref, z_ref, g_ref, on_ref, o_ref, s_ref, *, chunk, pairs):
    grp = pl.program_id(2)

    @pl.when(grp == 0)
    def _():
        s_ref[...] = jnp.zeros(s_ref.shape, F32)

    rows = q_ref.shape[0]
    masks = _block_masks(rows, chunk)
    heads = _gdn_chunk_parts(q_ref, k_ref, v_ref, [g_ref[pp] for pp in range(pairs)], masks)
    nh = len(heads)
    wbs = [h["w_mat"].astype(BF16) for h in heads]
    qdbs = [h["q_dec"].astype(BF16) for h in heads]
    kdbs = [h["k_dec"].astype(BF16) for h in heads]
    states = [s_ref[hv] for hv in range(nh)]
    v_new = [[] for _ in range(nh)]
    q_state = [[] for _ in range(nh)]
    for n in range(rows // chunk):
        rs = slice(n * chunk, (n + 1) * chunk)
        res = [_dot(jnp.concatenate([wb[rs], qdb[rs]], axis=0), st.astype(BF16))
               for wb, qdb, st in zip(wbs, qdbs, states)]
        for hv, h in enumerate(heads):
            v_n = h["u_mat"][rs] - res[hv][0:chunk]
            v_new[hv].append(v_n)
            q_state[hv].append(res[hv][chunk:2 * chunk])
            g_last = jnp.exp(h["g_tot"][n * chunk:n * chunk + 1, :])
            states[hv] = g_last * states[hv] + _dot_tn(kdbs[hv][rs], v_n.astype(BF16))
    onorm = on_ref[...]
    for hv, h in enumerate(heads):
        hcols = slice(hv * HEAD_DIM, (hv + 1) * HEAD_DIM)
        s_ref[hv] = states[hv]
        o = (jnp.concatenate(q_state[hv], axis=0)
             + _dot(h["qk_m"], jnp.concatenate(v_new[hv], axis=0).astype(BF16)))
        o_ref[:, hcols] = _gated_out(o, z_ref[:, hcols], onorm).astype(BF16)


def _gdn_prompt(qkv, qkvz, gates, o_norm, *, batch, seq_len):
    t = qkv.shape[0]
    r = PROMPT_GDN_ROWS
    groups = seq_len // r
    pairs = GDN_PAIRS
    hblocks = QK_HEADS // pairs
    qw, vw = pairs * HEAD_DIM, 2 * pairs * HEAD_DIM
    kern = functools.partial(_gdn_prompt_kernel, chunk=PROMPT_CHUNK, pairs=pairs)
    row = lambda b, h, g: b * groups + g
    return pl.pallas_call(
        kern,
        grid=(batch, hblocks, groups),
        in_specs=[
            pl.BlockSpec((r, qw), lambda b, h, g: (row(b, h, g), h)),
            pl.BlockSpec((r, qw), lambda b, h, g: (row(b, h, g), hblocks + h)),
            pl.BlockSpec((r, vw), lambda b, h, g: (row(b, h, g), hblocks + h)),
            pl.BlockSpec((r, vw), lambda b, h, g: (row(b, h, g), 2 * hblocks + h)),
            pl.BlockSpec((pairs, SUBLANES, r), lambda b, h, g: (h, 0, row(b, h, g))),
            pl.BlockSpec((1, HEAD_DIM), lambda b, h, g: (0, 0)),
        ],
        out_specs=[
            pl.BlockSpec((r, vw), lambda b, h, g: (row(b, h, g), h)),
            pl.BlockSpec((None, 2 * pairs, HEAD_DIM, HEAD_DIM), lambda b, h, g: (b, h, 0, 0)),
        ],
        out_shape=[
            jax.ShapeDtypeStruct((t, VAL_DIM), BF16),
            jax.ShapeDtypeStruct((batch, V_HEADS, HEAD_DIM, HEAD_DIM), F32),
        ],
        compiler_params=_params(("parallel", "parallel", "arbitrary")),
        name="gdn_prompt",
    )(qkv, qkv, qkv, qkvz, gates, o_norm)


def _gdn_sample_kernel(q_ref, k_ref, v_ref, z_ref, g_ref, on_ref, s0_ref, o_ref, s_ref,
                       lhs_scr, u_scr, kd_scr, gl_scr, vn_scr, os_scr, *, chunk):
    rows = q_ref.shape[0]
    nseq = rows // chunk
    masks = _block_masks(rows, chunk)
    heads = _gdn_chunk_parts(q_ref, k_ref, v_ref, [g_ref[...]], masks)
    for hh, h in enumerate(heads):
        lhs_scr[hh, :, 0:chunk, :] = h["w_mat"].reshape(nseq, chunk, HEAD_DIM)
        lhs_scr[hh, :, chunk:2 * chunk, :] = h["q_dec"].reshape(nseq, chunk, HEAD_DIM)
        u_scr[hh] = h["u_mat"]
        kd_scr[hh] = h["k_dec"]
        gl_scr[hh] = jnp.broadcast_to(jnp.exp(h["g_tot"]), (rows, HEAD_DIM))
    qk_masked = [h["qk_m"] for h in heads]
    onorm = on_ref[...]

    def per_sequences(step, carry):
        work = [(step * SEQ_UNROLL + lane, hh) for lane in range(SEQ_UNROLL) for hh in range(2)]
        rws = [pl.ds(pl.multiple_of(s * chunk, chunk), chunk) for s, _ in work]
        s0s = [s0_ref[s, hh] for s, hh in work]
        res = [_dot(lhs_scr[hh, s].astype(BF16), s0.astype(BF16))
               for (s, hh), s0 in zip(work, s0s)]
        v_new = [u_scr[hh, rw, :] - r[0:chunk] for (_, hh), rw, r in zip(work, rws, res)]
        upd = [_dot_tn(kd_scr[hh, rw, :], vn) for (_, hh), rw, vn in zip(work, rws, v_new)]
        for (s, hh), rw, s0, r, vn, up in zip(work, rws, s0s, res, v_new, upd):
            vn_scr[hh, rw, :] = vn
            os_scr[hh, rw, :] = r[chunk:2 * chunk]
            s_ref[s, hh] = gl_scr[hh, rw, :][0:1, :] * s0 + up
        return carry

    lax.fori_loop(0, nseq // SEQ_UNROLL, per_sequences, 0)
    for hh in range(2):
        hcols = slice(hh * HEAD_DIM, (hh + 1) * HEAD_DIM)
        o = os_scr[hh] + _dot(qk_masked[hh], vn_scr[hh].astype(BF16))
        o_ref[:, hcols] = _gated_out(o, z_ref[:, hcols], onorm).astype(BF16)


def _gdn_sample(qkv, qkvz, gates, o_norm, s0, *, chunk):
    t = qkv.shape[0]
    r = SAMPLE_GDN_ROWS
    nseq = r // chunk
    kern = functools.partial(_gdn_sample_kernel, chunk=chunk)
    return pl.pallas_call(
        kern,
        grid=(t // r, QK_HEADS),
        in_specs=[
            pl.BlockSpec((r, HEAD_DIM), lambda i, h: (i, h)),
            pl.BlockSpec((r, HEAD_DIM), lambda i, h: (i, QK_HEADS + h)),
            pl.BlockSpec((r, 2 * HEAD_DIM), lambda i, h: (i, QK_HEADS + h)),
            pl.BlockSpec((r, 2 * HEAD_DIM), lambda i, h: (i, 2 * QK_HEADS + h)),
            pl.BlockSpec((None, SUBLANES, r), lambda i, h: (h, 0, i)),
            pl.BlockSpec((1, HEAD_DIM), lambda i, h: (0, 0)),
            pl.BlockSpec((nseq, 2, HEAD_DIM, HEAD_DIM), lambda i, h: (i, h, 0, 0)),
        ],
        out_specs=[
            pl.BlockSpec((r, 2 * HEAD_DIM), lambda i, h: (i, h)),
            pl.BlockSpec((nseq, 2, HEAD_DIM, HEAD_DIM), lambda i, h: (i, h, 0, 0)),
        ],
        out_shape=[
            jax.ShapeDtypeStruct((t, VAL_DIM), BF16),
            jax.ShapeDtypeStruct(s0.shape, F32),
        ],
        scratch_shapes=[
            pltpu.VMEM((2, nseq, 2 * chunk, HEAD_DIM), F32),
            pltpu.VMEM((2, r, HEAD_DIM), F32),
            pltpu.VMEM((2, r, HEAD_DIM), F32),
            pltpu.VMEM((2, r, HEAD_DIM), F32),
            pltpu.VMEM((2, r, HEAD_DIM), F32),
            pltpu.VMEM((2, r, HEAD_DIM), F32),
        ],
        compiler_params=_params(("parallel", "parallel")),
        name="gdn_sample",
    )(qkv, qkv, qkv, qkvz, gates, o_norm, s0)


def _front_pad_rows(state, rows):
    b, w, c = state.shape
    return jnp.pad(state, ((0, 0), (0, rows - w), (0, 0))).reshape(b * rows, c)


def _trunk(x, conv_a, conv_b, ssm_b, wts, *, grouped):
    batch, seq_len, _ = x.shape
    t = batch * seq_len
    x0 = x.reshape(t, D_MODEL)
    if grouped:
        state_a = _front_pad_rows(conv_a, SUBLANES)
        state_b = _front_pad_rows(conv_b, SUBLANES)
        chunk = seq_len
    else:
        state_a = jnp.zeros((SUBLANES, D_MODEL), F32)
        state_b = None
        chunk = PROMPT_CHUNK

    u, y = _mixer_a(x0, wts["norm_a"], wts["w_in_a"], wts["w_conv_a"], state_a,
                    seq_len=seq_len, grouped=grouped)
    new_conv_a = u.reshape(batch, seq_len, D_MODEL)[:, seq_len - (CONV_A_WIDTH - 1):]
    x1 = _matmul_residual(y, wts["w_out_a"], x0)
    x2 = _mlp(x1, wts["mlp_norm"], wts["w_up"], wts["w_down"], wts["final_norm"], layer=0,
              final_norm=False)

    qkvz, gates = _gdn_in(x2, wts["norm_b"], wts["w_in_b_t"], wts["a_log"], wts["dt_bias"],
                          chunk=chunk)
    new_conv_b = qkvz.reshape(batch, seq_len, QKVZ_DIM)[:, seq_len - (CONV_B_WIDTH - 1):, :CONV_DIM]
    qkv = _gdn_conv(qkvz, state_b, wts["w_conv_b"], seq_len=seq_len, grouped=grouped)
    if grouped:
        o, s_new = _gdn_sample(qkv, qkvz, gates, wts["o_norm"], ssm_b, chunk=chunk)
    else:
        o, s_new = _gdn_prompt(qkv, qkvz, gates, wts["o_norm"], batch=batch, seq_len=seq_len)
    x3 = _matmul_residual(o, wts["w_out_b"], x2)
    x4 = _mlp(x3, wts["mlp_norm"], wts["w_up"], wts["w_down"], wts["final_norm"], layer=1,
              final_norm=True)
    return (x4.reshape(batch, seq_len, D_MODEL), new_conv_a[None], new_conv_b[None], s_new[None])


def kernel(x_prompt, x_sample, state_conv_a, state_conv_b, state_ssm_b, norm_a, w_in_a, w_conv_a,
           w_out_a, norm_b, w_in_b, w_conv_b, a_log_b, dt_bias_b, o_norm_b, w_out_b, mlp_norm,
           w_up, w_down, final_norm):
    wts = {
        "norm_a": norm_a[0][None],
        "w_in_a": w_in_a[0].astype(BF16),
        "w_conv_a": w_conv_a[0],
        "w_out_a": w_out_a[0].astype(BF16),
        "norm_b": norm_b[0][None],
        "w_in_b_t": w_in_b[0].T.astype(BF16),
        "w_conv_b": w_conv_b[0],
        "a_log": a_log_b[0][:, None],
        "dt_bias": dt_bias_b[0][:, None],
        "o_norm": o_norm_b[0][None],
        "w_out_b": w_out_b[0].astype(BF16),
        "mlp_norm": mlp_norm[:, None, :],
        "w_up": w_up.astype(BF16),
        "w_down": w_down.astype(BF16),
        "final_norm": final_norm[None],
    }
    y_p, ca_p, cb_p, s_p = _trunk(x_prompt, None, None, None, wts, grouped=False)
    y_s, ca_s, cb_s, s_s = _trunk(x_sample, state_conv_a[0], state_conv_b[0], state_ssm_b[0], wts,
                                  grouped=True)
    return (y_p, y_s, ca_p, cb_p, s_p, ca_s, cb_s, s_s)
```

```python
import functools

import jax
import jax.numpy as jnp
from jax import lax
from jax.experimental import pallas as pl
from jax.experimental.pallas import tpu as pltpu

D_MODEL = 2048
D_FF = 4 * D_MODEL
HEAD_DIM = 128
QK_HEADS = D_MODEL // HEAD_DIM
V_HEADS = 2 * QK_HEADS
KEY_DIM = QK_HEADS * HEAD_DIM
VAL_DIM = V_HEADS * HEAD_DIM
CONV_DIM = 2 * KEY_DIM + VAL_DIM
QKVZ_DIM = CONV_DIM + VAL_DIM
CONV_A_WIDTH = 3
CONV_B_WIDTH = 4
PROMPT_CHUNK = 64
EPS = 1e-6

F32 = jnp.float32
BF16 = jnp.bfloat16

SUBLANES = 8
MXU_COLS = 256
ROW_TILE = 1024
CONV_TILE = 1024
CONV_STRIP = 32
PROMPT_GDN_ROWS = 128
SAMPLE_GDN_ROWS = 256
GDN_PAIRS = 8
SEQ_UNROLL = 32
VMEM_LIMIT = 56 * 1024 * 1024


def _params(semantics):
    return pltpu.CompilerParams(dimension_semantics=semantics, vmem_limit_bytes=VMEM_LIMIT)


def _rmsnorm(x, w):
    ms = jnp.mean(x * x, axis=-1, keepdims=True)
    return (x * lax.rsqrt(ms + EPS)) * w


def _dot(a, b):
    return jnp.dot(a, b, preferred_element_type=F32)


def _dot_nt(a, b):
    return lax.dot_general(a, b, (((1,), (1,)), ((), ())), preferred_element_type=F32)


def _dot_tn(a, b):
    return lax.dot_general(a, b, (((0,), (0,)), ((), ())), preferred_element_type=F32)


def _sigmoid(x):
    return 1.0 / (1.0 + jnp.exp(-x))


def _softplus(x):
    return jnp.maximum(x, 0.0) + jnp.log1p(jnp.exp(-jnp.abs(x)))


def _prev_rows_seq(u, halo, j):
    tm = u.shape[0]
    return pltpu.roll(jnp.concatenate([u, halo], axis=0), j, 0)[0:tm]


def _state_rows_grouped(shifted, state, j, width):
    tm = shifted.shape[0]
    rmod = lax.broadcasted_iota(jnp.int32, shifted.shape, 0) & (SUBLANES - 1)
    back = width - 1 - j
    st = state if back == 0 else pltpu.roll(state, tm - back, 0)
    return jnp.where(rmod < j, st, shifted)


def _prev_rows_grouped(u, state, j, width):
    return _state_rows_grouped(pltpu.roll(u, j, 0), state, j, width)


def _mixer_a_kernel(x_ref, nw_ref, wb_ref, wc_ref, wx_ref, cw_ref, st_ref, u_ref, y_ref,
                    hs_ref, carry_ref, *, tiles_per_seq, grouped):
    i = pl.program_id(0)
    j = pl.program_id(1)

    @pl.when(j == 0)
    def _():
        hs_ref[...] = _rmsnorm(x_ref[...], nw_ref[...]).astype(BF16)

    hs = hs_ref[...]
    tm = hs.shape[0]
    for c in range(wb_ref.shape[1] // MXU_COLS):
        cols = slice(c * MXU_COLS, (c + 1) * MXU_COLS)
        gate_b = _dot(hs, wb_ref[:, cols])
        u = _dot(hs, wc_ref[:, cols]) * _dot(hs, wx_ref[:, cols])
        u_ref[:, cols] = u
        if grouped:
            state = st_ref[:, cols]
            prev1 = _prev_rows_grouped(u, state, 1, CONV_A_WIDTH)
            prev2 = _prev_rows_grouped(u, state, 2, CONV_A_WIDTH)
        else:
            halo = jnp.where(i % tiles_per_seq != 0, carry_ref[j, :, cols], 0.0)
            prev1 = _prev_rows_seq(u, halo, 1)
            prev2 = _prev_rows_seq(u, halo, 2)
            carry_ref[j, :, cols] = u[tm - SUBLANES:tm]
        y = prev2 * cw_ref[0:1, cols] + prev1 * cw_ref[1:2, cols] + u * cw_ref[2:3, cols]
        y_ref[:, cols] = (gate_b * y).astype(BF16)


def _mixer_a(x, norm_w, w_in, conv_w, state, *, seq_len, grouped):
    t = x.shape[0]
    tm, tn = ROW_TILE, 512
    nj = D_MODEL // tn
    if grouped:
        st_spec = pl.BlockSpec((tm, tn), lambda i, j: (i, j))
    else:
        st_spec = pl.BlockSpec((SUBLANES, tn), lambda i, j: (0, j))
    kern = functools.partial(_mixer_a_kernel, tiles_per_seq=max(seq_len // tm, 1), grouped=grouped)
    return pl.pallas_call(
        kern,
        grid=(t // tm, nj),
        in_specs=[
            pl.BlockSpec((tm, D_MODEL), lambda i, j: (i, 0)),
            pl.BlockSpec((1, D_MODEL), lambda i, j: (0, 0)),
            pl.BlockSpec((D_MODEL, tn), lambda i, j: (0, j)),
            pl.BlockSpec((D_MODEL, tn), lambda i, j: (0, nj + j)),
            pl.BlockSpec((D_MODEL, tn), lambda i, j: (0, 2 * nj + j)),
            pl.BlockSpec((CONV_A_WIDTH, tn), lambda i, j: (0, j)),
            st_spec,
        ],
        out_specs=[
            pl.BlockSpec((tm, tn), lambda i, j: (i, j)),
            pl.BlockSpec((tm, tn), lambda i, j: (i, j)),
        ],
        out_shape=[
            jax.ShapeDtypeStruct((t, D_MODEL), F32),
            jax.ShapeDtypeStruct((t, D_MODEL), BF16),
        ],
        scratch_shapes=[
            pltpu.VMEM((tm, D_MODEL), BF16),
            pltpu.VMEM((nj, SUBLANES, tn), F32),
        ],
        compiler_params=_params(("arbitrary", "arbitrary")),
        name="mixer_a",
    )(x, norm_w, w_in, w_in, w_in, conv_w, state)


def _matmul_residual_kernel(a_ref, w_ref, r_ref, o_ref):
    o_ref[...] = r_ref[...] + _dot(a_ref[...], w_ref[...])


def _matmul_residual(a, w, res):
    t, k = a.shape
    n = w.shape[1]
    tm, tn = ROW_TILE, 1024
    return pl.pallas_call(
        _matmul_residual_kernel,
        grid=(t // tm, n // tn),
        in_specs=[
            pl.BlockSpec((tm, k), lambda i, j: (i, 0)),
            pl.BlockSpec((k, tn), lambda i, j: (0, j)),
            pl.BlockSpec((tm, tn), lambda i, j: (i, j)),
        ],
        out_specs=pl.BlockSpec((tm, tn), lambda i, j: (i, j)),
        out_shape=jax.ShapeDtypeStruct((t, n), F32),
        compiler_params=_params(("parallel", "arbitrary")),
        name="matmul_residual",
    )(a, w, res)


def _mlp_kernel(x_ref, nw_ref, wu_ref, wd_ref, fw_ref, o_ref, hs_ref, *, final_norm):
    j = pl.program_id(1)

    @pl.when(j == 0)
    def _():
        x = x_ref[...]
        hs_ref[...] = _rmsnorm(x, nw_ref[...]).astype(BF16)
        o_ref[...] = x

    a = _dot(hs_ref[...], wu_ref[...])
    a = jnp.square(jnp.maximum(a, 0.0)).astype(BF16)
    o_ref[...] += _dot(a, wd_ref[...])

    if final_norm:
        @pl.when(j == pl.num_programs(1) - 1)
        def _():
            o_ref[...] = _rmsnorm(o_ref[...], fw_ref[...])


def _mlp(x, norm_w, w_up, w_down, final_w, *, layer, final_norm):
    t = x.shape[0]
    tm, tf = ROW_TILE, 512
    kern = functools.partial(_mlp_kernel, final_norm=final_norm)
    return pl.pallas_call(
        kern,
        grid=(t // tm, D_FF // tf),
        in_specs=[
            pl.BlockSpec((tm, D_MODEL), lambda i, j: (i, 0)),
            pl.BlockSpec((None, 1, D_MODEL), lambda i, j: (layer, 0, 0)),
            pl.BlockSpec((None, D_MODEL, tf), lambda i, j: (layer, 0, j)),
            pl.BlockSpec((None, tf, D_MODEL), lambda i, j: (layer, j, 0)),
            pl.BlockSpec((1, D_MODEL), lambda i, j: (0, 0)),
        ],
        out_specs=pl.BlockSpec((tm, D_MODEL), lambda i, j: (i, 0)),
        out_shape=jax.ShapeDtypeStruct((t, D_MODEL), F32),
        scratch_shapes=[pltpu.VMEM((tm, D_MODEL), BF16)],
        compiler_params=_params(("parallel", "arbitrary")),
        name="mlp",
    )(x, norm_w, w_up, w_down, final_w)


def _gdn_in_kernel(x_ref, nw_ref, w_ref, wg_ref, alog_ref, dtb_ref, qkvz_ref, gates_ref, hs_ref,
                   *, chunk):
    j = pl.program_id(1)

    @pl.when(j == 0)
    def _():
        hs = _rmsnorm(x_ref[...], nw_ref[...]).astype(BF16)
        hs_ref[...] = hs
        raw = _dot_nt(wg_ref[...], hs)
        tm = raw.shape[1]
        beta = _sigmoid(raw[0:V_HEADS])
        g = -jnp.exp(alog_ref[...]) * _softplus(raw[V_HEADS:2 * V_HEADS] + dtb_ref[...])
        pos = lax.broadcasted_iota(jnp.int32, g.shape, 1) & (chunk - 1)
        csum = g
        ssum = g
        s = 1
        while s < chunk:
            csum = csum + jnp.where(pos >= s, pltpu.roll(csum, s, 1), 0.0)
            ssum = ssum + jnp.where(pos < chunk - s, pltpu.roll(ssum, tm - s, 1), 0.0)
            s *= 2
        rest = ssum - g
        zero2 = jnp.zeros((2, tm), F32)
        for p in range(QK_HEADS):
            gates_ref[p, 0:2, :] = beta[2 * p:2 * p + 2]
            gates_ref[p, 2:4, :] = csum[2 * p:2 * p + 2]
            gates_ref[p, 4:6, :] = rest[2 * p:2 * p + 2]
            gates_ref[p, 6:8, :] = zero2

    qkvz_ref[...] = _dot_nt(hs_ref[...], w_ref[...])


def _gdn_in(x, norm_w, w_in_t, a_log_col, dt_bias_col, *, chunk):
    t = x.shape[0]
    tm, tn = ROW_TILE, 1536
    kern = functools.partial(_gdn_in_kernel, chunk=chunk)
    return pl.pallas_call(
        kern,
        grid=(t // tm, QKVZ_DIM // tn),
        in_specs=[
            pl.BlockSpec((tm, D_MODEL), lambda i, j: (i, 0)),
            pl.BlockSpec((1, D_MODEL), lambda i, j: (0, 0)),
            pl.BlockSpec((tn, D_MODEL), lambda i, j: (j, 0)),
            pl.BlockSpec((2 * V_HEADS, D_MODEL), lambda i, j: (QKVZ_DIM // (2 * V_HEADS), 0)),
            pl.BlockSpec((V_HEADS, 1), lambda i, j: (0, 0)),
            pl.BlockSpec((V_HEADS, 1), lambda i, j: (0, 0)),
        ],
        out_specs=[
            pl.BlockSpec((tm, tn), lambda i, j: (i, j)),
            pl.BlockSpec((QK_HEADS, SUBLANES, tm), lambda i, j: (0, 0, i)),
        ],
        out_shape=[
            jax.ShapeDtypeStruct((t, QKVZ_DIM), F32),
            jax.ShapeDtypeStruct((QK_HEADS, SUBLANES, t), F32),
        ],
        scratch_shapes=[pltpu.VMEM((tm, D_MODEL), BF16)],
        compiler_params=_params(("parallel", "arbitrary")),
        name="gdn_in",
    )(x, norm_w, w_in_t, w_in_t, a_log_col, dt_bias_col)


def _gdn_conv_kernel(x_ref, st_ref, cw_ref, o_ref, slab_ref, *, tiles_per_seq, grouped, tn):
    i = pl.program_id(0)
    j = pl.program_id(1)
    tm = x_ref.shape[0]
    n_qk = 2 * KEY_DIM // tn

    def run(normalise):
        scale = jnp.where(j < KEY_DIM // tn, HEAD_DIM ** -0.5, 1.0).astype(F32)
        for hd in range(tn // HEAD_DIM):
            cols = slice(hd * HEAD_DIM, (hd + 1) * HEAD_DIM)
            taps = [cw_ref[s:s + 1, cols] for s in range(CONV_B_WIDTH)]
            if grouped:
                slab_ref[0:SUBLANES, :] = jnp.zeros((SUBLANES, HEAD_DIM), F32)
            else:
                slab_ref[0:SUBLANES, :] = jnp.where(i % tiles_per_seq != 0, st_ref[:, cols], 0.0)
            slab_ref[SUBLANES:SUBLANES + tm, :] = x_ref[:, cols]

            def strip(rb, carry):
                r0 = pl.multiple_of(rb * CONV_STRIP, CONV_STRIP)
                u = slab_ref[pl.ds(r0 + SUBLANES, CONV_STRIP), :]
                prev = [slab_ref[pl.ds(r0 + SUBLANES - s, CONV_STRIP), :] for s in (1, 2, 3)]
                if grouped:
                    state = st_ref[pl.ds(r0, CONV_STRIP), cols]
                    prev = [_state_rows_grouped(prev[s - 1], state, s, CONV_B_WIDTH) for s in (1, 2, 3)]
                y = prev[2] * taps[0] + prev[1] * taps[1] + prev[0] * taps[2] + u * taps[3]
                o_ref[pl.ds(r0, CONV_STRIP), cols] = y * _sigmoid(y)
                return carry

            lax.fori_loop(0, tm // CONV_STRIP, strip, 0, unroll=True)
            if normalise:
                y = o_ref[:, cols]
                ss = jnp.sum(y * y, axis=-1, keepdims=True)
                o_ref[:, cols] = (y * lax.rsqrt(ss + EPS)) * scale

    @pl.when(j >= n_qk)
    def _():
        run(False)

    @pl.when(j < n_qk)
    def _():
        run(True)


def _gdn_conv(qkvz, state, conv_w, *, seq_len, grouped):
    t = qkvz.shape[0]
    tm, tn = CONV_TILE, 1024
    if grouped:
        st_spec = pl.BlockSpec((tm, tn), lambda i, j: (i, j))
        st = state
    else:
        blocks_per_tile = tm // SUBLANES
        st_spec = pl.BlockSpec((SUBLANES, tn), lambda i, j: (jnp.maximum(i * blocks_per_tile - 1, 0), j))
        st = qkvz
    kern = functools.partial(_gdn_conv_kernel, tiles_per_seq=max(seq_len // tm, 1), grouped=grouped, tn=tn)
    return pl.pallas_call(
        kern,
        grid=(t // tm, CONV_DIM // tn),
        in_specs=[
            pl.BlockSpec((tm, tn), lambda i, j: (i, j)),
            st_spec,
            pl.BlockSpec((CONV_B_WIDTH, tn), lambda i, j: (0, j)),
        ],
        out_specs=pl.BlockSpec((tm, tn), lambda i, j: (i, j)),
        out_shape=jax.ShapeDtypeStruct((t, CONV_DIM), F32),
        scratch_shapes=[pltpu.VMEM((SUBLANES + tm, HEAD_DIM), F32)],
        compiler_params=_params(("parallel", "parallel")),
        name="gdn_conv",
    )(qkvz, st, conv_w)


def _block_masks(rows, chunk):
    levels = chunk.bit_length() - 1
    ri = lax.broadcasted_iota(jnp.int32, (rows, rows), 0)
    ci = lax.broadcasted_iota(jnp.int32, (rows, rows), 1)
    same = [None] + [lax.shift_right_logical(ri, lb) == lax.shift_right_logical(ci, lb)
                     for lb in range(1, levels + 1)]
    return {
        "diag": ri == ci,
        "pair": same[1],
        "join": [same[lb + 1] & ~same[lb] for lb in range(1, levels)],
        "low": same[levels] & (ri >= ci),
        "strict": same[levels] & (ri > ci),
    }


def _gate_cols(g):
    rows = g.shape[1]
    padded = jnp.concatenate([g, jnp.zeros((HEAD_DIM - SUBLANES, rows), F32)], axis=0)
    return padded.T


def _gdn_chunk_parts(q_ref, k_ref, v_ref, gates, masks):
    heads = []
    for pp, grow in enumerate(gates):
        qcols = slice(pp * HEAD_DIM, (pp + 1) * HEAD_DIM)
        q = q_ref[:, qcols]
        k = k_ref[:, qcols]
        kb = k.astype(BF16)
        kk = _dot_nt(kb, kb)
        qk = _dot_nt(q.astype(BF16), kb)
        cols = _gate_cols(grow)
        for hh in range(2):
            hv = 2 * pp + hh
            beta_c = cols[:, hh:hh + 1]
            g_c = cols[:, 2 + hh:3 + hh]
            x_c = cols[:, 4 + hh:5 + hh]
            g_r = grow[2 + hh:3 + hh, :]
            decay = jnp.exp(jnp.where(masks["low"], g_c - g_r, -jnp.inf))
            eg = jnp.exp(g_c)
            v = v_ref[:, hv * HEAD_DIM:(hv + 1) * HEAD_DIM]
            heads.append({
                "l_mat": jnp.where(masks["strict"], beta_c * kk * decay, 0.0).astype(BF16),
                "rhs": jnp.concatenate([v * beta_c, k * (beta_c * eg)], axis=1).astype(BF16),
                "qk_m": (qk * decay).astype(BF16),
                "q_dec": q * eg,
                "k_dec": k * jnp.exp(x_c),
                "g_tot": g_c + x_c,
            })
    invs = [jnp.where(masks["diag"], 1.0, jnp.where(masks["pair"], -h["l_mat"], 0.0)) for h in heads]
    for join in masks["join"]:
        cross = [_dot(jnp.where(join, h["l_mat"], 0.0), inv).astype(BF16)
                 for h, inv in zip(heads, invs)]
        invs = [inv - _dot(inv, cr).astype(BF16) for inv, cr in zip(invs, cross)]
    for h, inv in zip(heads, invs):
        uw = _dot(inv, h["rhs"])
        h["u_mat"] = uw[:, :HEAD_DIM]
        h["w_mat"] = uw[:, HEAD_DIM:]
    return heads


def _gated_out(o, z, onorm):
    ms = jnp.mean(o * o, axis=-1, keepdims=True)
    return ((o * lax.rsqrt(ms + EPS)) * onorm) * (z * _sigmoid(z))


def _gdn_prompt_kernel(q_ref, k_ref, v_ref, z_ref, g_ref, on_ref, o_ref, s_ref, *, chunk, pairs):
    grp = pl.program_id(2)

    @pl.when(grp == 0)
    def _():
        s_ref[...] = jnp.zeros(s_ref.shape, F32)

    rows = q_ref.shape[0]
    masks = _block_masks(rows, chunk)
    heads = _gdn_chunk_parts(q_ref, k_ref, v_ref, [g_ref[pp] for pp in range(pairs)], masks)
    nh = len(heads)
    wbs = [h["w_mat"].astype(BF16) for h in heads]
    qdbs = [h["q_dec"].astype(BF16) for h in heads]
    kdbs = [h["k_dec"].astype(BF16) for h in heads]
    states = [s_ref[hv] for hv in range(nh)]
    v_new = [[] for _ in range(nh)]
    q_state = [[] for _ in range(nh)]
    for n in range(rows // chunk):
        rs = slice(n * chunk, (n + 1) * chunk)
        res = [_dot(jnp.concatenate([wb[rs], qdb[rs]], axis=0), st.astype(BF16))
               for wb, qdb, st in zip(wbs, qdbs, states)]
        for hv, h in enumerate(heads):
            v_n = h["u_mat"][rs] - res[hv][0:chunk]
            v_new[hv].append(v_n)
            q_state[hv].append(res[hv][chunk:2 * chunk])
            g_last = jnp.exp(h["g_tot"][n * chunk:n * chunk + 1, :])
            states[hv] = g_last * states[hv] + _dot_tn(kdbs[hv][rs], v_n.astype(BF16))
    onorm = on_ref[...]
    for hv, h in enumerate(heads):
        hcols = slice(hv * HEAD_DIM, (hv + 1) * HEAD_DIM)
        s_ref[hv] = states[hv]
        o = (jnp.concatenate(q_state[hv], axis=0)
             + _dot(h["qk_m"], jnp.concatenate(v_new[hv], axis=0).astype(BF16)))
        o_ref[:, hcols] = _gated_out(o, z_ref[:, hcols], onorm).astype(BF16)


def _gdn_prompt(qkv, qkvz, gates, o_norm, *, batch, seq_len):
    t = qkv.shape[0]
    r = PROMPT_GDN_ROWS
    groups = seq_len // r
    pairs = GDN_PAIRS
    hblocks = QK_HEADS // pairs
    qw, vw = pairs * HEAD_DIM, 2 * pairs * HEAD_DIM
    kern = functools.partial(_gdn_prompt_kernel, chunk=PROMPT_CHUNK, pairs=pairs)
    row = lambda b, h, g: b * groups + g
    return pl.pallas_call(
        kern,
        grid=(batch, hblocks, groups),
        in_specs=[
            pl.BlockSpec((r, qw), lambda b, h, g: (row(b, h, g), h)),
            pl.BlockSpec((r, qw), lambda b, h, g: (row(b, h, g), hblocks + h)),
            pl.BlockSpec((r, vw), lambda b, h, g: (row(b, h, g), hblocks + h)),
            pl.BlockSpec((r, vw), lambda b, h, g: (row(b, h, g), 2 * hblocks + h)),
            pl.BlockSpec((pairs, SUBLANES, r), lambda b, h, g: (h, 0, row(b, h, g))),
            pl.BlockSpec((1, HEAD_DIM), lambda b, h, g: (0, 0)),
        ],
        out_specs=[
            pl.BlockSpec((r, vw), lambda b, h, g: (row(b, h, g), h)),
            pl.BlockSpec((None, 2 * pairs, HEAD_DIM, HEAD_DIM), lambda b, h, g: (b, h, 0, 0)),
        ],
        out_shape=[
            jax.ShapeDtypeStruct((t, VAL_DIM), BF16),
            jax.ShapeDtypeStruct((batch, V_HEADS, HEAD_DIM, HEAD_DIM), F32),
        ],
        compiler_params=_params(("parallel", "parallel", "arbitrary")),
        name="gdn_prompt",
    )(qkv, qkv, qkv, qkvz, gates, o_norm)


def _gdn_sample_kernel(q_ref, k_ref, v_ref, z_ref, g_ref, on_ref, s0_ref, o_ref, s_ref,
                       lhs_scr, u_scr, kd_scr, gl_scr, vn_scr, os_scr, *, chunk):
    rows = q_ref.shape[0]
    nseq = rows // chunk
    masks = _block_masks(rows, chunk)
    heads = _gdn_chunk_parts(q_ref, k_ref, v_ref, [g_ref[...]], masks)
    for hh, h in enumerate(heads):
        lhs_scr[hh, :, 0:chunk, :] = h["w_mat"].reshape(nseq, chunk, HEAD_DIM)
        lhs_scr[hh, :, chunk:2 * chunk, :] = h["q_dec"].reshape(nseq, chunk, HEAD_DIM)
        u_scr[hh] = h["u_mat"]
        kd_scr[hh] = h["k_dec"]
        gl_scr[hh] = jnp.broadcast_to(jnp.exp(h["g_tot"]), (rows, HEAD_DIM))
    qk_masked = [h["qk_m"] for h in heads]
    onorm = on_ref[...]

    def per_sequences(step, carry):
        work = [(step * SEQ_UNROLL + lane, hh) for lane in range(SEQ_UNROLL) for hh in range(2)]
        rws = [pl.ds(pl.multiple_of(s * chunk, chunk), chunk) for s, _ in work]
        s0s = [s0_ref[s, hh] for s, hh in work]
        res = [_dot(lhs_scr[hh, s].astype(BF16), s0.astype(BF16))
               for (s, hh), s0 in zip(work, s0s)]
        v_new = [u_scr[hh, rw, :] - r[0:chunk] for (_, hh), rw, r in zip(work, rws, res)]
        upd = [_dot_tn(kd_scr[hh, rw, :], vn) for (_, hh), rw, vn in zip(work, rws, v_new)]
        for (s, hh), rw, s0, r, vn, up in zip(work, rws, s0s, res, v_new, upd):
            vn_scr[hh, rw, :] = vn
            os_scr[hh, rw, :] = r[chunk:2 * chunk]
            s_ref[s, hh] = gl_scr[hh, rw, :][0:1, :] * s0 + up
        return carry

    lax.fori_loop(0, nseq // SEQ_UNROLL, per_sequences, 0)
    for hh in range(2):
        hcols = slice(hh * HEAD_DIM, (hh + 1) * HEAD_DIM)
        o = os_scr[hh] + _dot(qk_masked[hh], vn_scr[hh].astype(BF16))
        o_ref[:, hcols] = _gated_out(o, z_ref[:, hcols], onorm).astype(BF16)


def _gdn_sample(qkv, qkvz, gates, o_norm, s0, *, chunk):
    t = qkv.shape[0]
    r = SAMPLE_GDN_ROWS
    nseq = r // chunk
    kern = functools.partial(_gdn_sample_kernel, chunk=chunk)
    return pl.pallas_call(
        kern,
        grid=(t // r, QK_HEADS),
        in_specs=[
            pl.BlockSpec((r, HEAD_DIM), lambda i, h: (i, h)),
            pl.BlockSpec((r, HEAD_DIM), lambda i, h: (i, QK_HEADS + h)),
            pl.BlockSpec((r, 2 * HEAD_DIM), lambda i, h: (i, QK_HEADS + h)),
            pl.BlockSpec((r, 2 * HEAD_DIM), lambda i, h: (i, 2 * QK_HEADS + h)),
            pl.BlockSpec((None, SUBLANES, r), lambda i, h: (h, 0, i)),
            pl.BlockSpec((1, HEAD_DIM), lambda i, h: (0, 0)),
            pl.BlockSpec((nseq, 2, HEAD_DIM, HEAD_DIM), lambda i, h: (i, h, 0, 0)),
        ],
        out_specs=[
            pl.BlockSpec((r, 2 * HEAD_DIM), lambda i, h: (i, h)),
            pl.BlockSpec((nseq, 2, HEAD_DIM, HEAD_DIM), lambda i, h: (i, h, 0, 0)),
        ],
        out_shape=[
            jax.ShapeDtypeStruct((t, VAL_DIM), BF16),
            jax.ShapeDtypeStruct(s0.shape, F32),
        ],
        scratch_shapes=[
            pltpu.VMEM((2, nseq, 2 * chunk, HEAD_DIM), F32),
            pltpu.VMEM((2, r, HEAD_DIM), F32),
            pltpu.VMEM((2, r, HEAD_DIM), F32),
            pltpu.VMEM((2, r, HEAD_DIM), F32),
            pltpu.VMEM((2, r, HEAD_DIM), F32),
            pltpu.VMEM((2, r, HEAD_DIM), F32),
        ],
        compiler_params=_params(("parallel", "parallel")),
        name="gdn_sample",
    )(qkv, qkv, qkv, qkvz, gates, o_norm, s0)


def _front_pad_rows(state, rows):
    b, w, c = state.shape
    return jnp.pad(state, ((0, 0), (0, rows - w), (0, 0))).reshape(b * rows, c)


def _trunk(x, conv_a, conv_b, ssm_b, wts, *, grouped):
    batch, seq_len, _ = x.shape
    t = batch * seq_len
    x0 = x.reshape(t, D_MODEL)
    if grouped:
        state_a = _front_pad_rows(conv_a, SUBLANES)
        state_b = _front_pad_rows(conv_b, SUBLANES)
        chunk = seq_len
    else:
        state_a = jnp.zeros((SUBLANES, D_MODEL), F32)
        state_b = None
        chunk = PROMPT_CHUNK

    u, y = _mixer_a(x0, wts["norm_a"], wts["w_in_a"], wts["w_conv_a"], state_a,
                    seq_len=seq_len, grouped=grouped)
    new_conv_a = u.reshape(batch, seq_len, D_MODEL)[:, seq_len - (CONV_A_WIDTH - 1):]
    x1 = _matmul_residual(y, wts["w_out_a"], x0)
    x2 = _mlp(x1, wts["mlp_norm"], wts["w_up"], wts["w_down"], wts["final_norm"], layer=0,
              final_norm=False)

    qkvz, gates = _gdn_in(x2, wts["norm_b"], wts["w_in_b_t"], wts["a_log"], wts["dt_bias"],
                          chunk=chunk)
    new_conv_b = qkvz.reshape(batch, seq_len, QKVZ_DIM)[:, seq_len - (CONV_B_WIDTH - 1):, :CONV_DIM]
    qkv = _gdn_conv(qkvz, state_b, wts["w_conv_b"], seq_len=seq_len, grouped=grouped)
    if grouped:
        o, s_new = _gdn_sample(qkv, qkvz, gates, wts["o_norm"], ssm_b, chunk=chunk)
    else:
        o, s_new = _gdn_prompt(qkv, qkvz, gates, wts["o_norm"], batch=batch, seq_len=seq_len)
    x3 = _matmul_residual(o, wts["w_out_b"], x2)
    x4 = _mlp(x3, wts["mlp_norm"], wts["w_up"], wts["w_down"], wts["final_norm"], layer=1,
              final_norm=True)
    return (x4.reshape(batch, seq_len, D_MODEL), new_conv_a[None], new_conv_b[None], s_new[None])


def kernel(x_prompt, x_sample, state_conv_a, state_conv_b, state_ssm_b, norm_a, w_in_a, w_conv_a,
           w_out_a, norm_b, w_in_b, w_conv_b, a_log_b, dt_bias_b, o_norm_b, w_out_b, mlp_norm,
           w_up, w_down, final_norm):
    wts = {
        "norm_a": norm_a[0][None],
        "w_in_a": w_in_a[0].astype(BF16),
        "w_conv_a": w_conv_a[0],
        "w_out_a": w_out_a[0].astype(BF16),
        "norm_b": norm_b[0][None],
        "w_in_b_t": w_in_b[0].T.astype(BF16),
        "w_conv_b": w_conv_b[0],
        "a_log": a_log_b[0][:, None],
        "dt_bias": dt_bias_b[0][:, None],
        "o_norm": o_norm_b[0][None],
        "w_out_b": w_out_b[0].astype(BF16),
        "mlp_norm": mlp_norm[:, None, :],
        "w_up": w_up.astype(BF16),
        "w_down": w_down.astype(BF16),
        "final_norm": final_norm[None],
    }
    y_p, ca_p, cb_p, s_p = _trunk(x_prompt, None, None, None, wts, grouped=False)
    y_s, ca_s, cb_s, s_s = _trunk(x_sample, state_conv_a[0], state_conv_b[0], state_ssm_b[0], wts,
                                  grouped=True)
    return (y_p, y_s, ca_p, cb_p, s_p, ca_s, cb_s, s_s)
```

```python
import functools

import jax
import jax.numpy as jnp
from jax import lax
from jax.experimental import pallas as pl
from jax.experimental.pallas import tpu as pltpu

D_MODEL = 2048
D_FF = 4 * D_MODEL
HEAD_DIM = 128
QK_HEADS = D_MODEL // HEAD_DIM
V_HEADS = 2 * QK_HEADS
KEY_DIM = QK_HEADS * HEAD_DIM
VAL_DIM = V_HEADS * HEAD_DIM
CONV_DIM = 2 * KEY_DIM + VAL_DIM
QKVZ_DIM = CONV_DIM + VAL_DIM
CONV_A_WIDTH = 3
CONV_B_WIDTH = 4
PROMPT_CHUNK = 64
EPS = 1e-6

F32 = jnp.float32
BF16 = jnp.bfloat16

SUBLANES = 8
MXU_COLS = 256
ROW_TILE = 1024
CONV_TILE = 1024
CONV_STRIP = 32
PROMPT_GDN_ROWS = 128
SAMPLE_GDN_ROWS = 256
GDN_PAIRS = 8
SEQ_UNROLL = 32
VMEM_LIMIT = 56 * 1024 * 1024


def _params(semantics):
    return pltpu.CompilerParams(dimension_semantics=semantics, vmem_limit_bytes=VMEM_LIMIT)


def _rmsnorm(x, w):
    ms = jnp.mean(x * x, axis=-1, keepdims=True)
    return (x * lax.rsqrt(ms + EPS)) * w


def _dot(a, b):
    return jnp.dot(a, b, preferred_element_type=F32)


def _dot_nt(a, b):
    return lax.dot_general(a, b, (((1,), (1,)), ((), ())), preferred_element_type=F32)


def _dot_tn(a, b):
    return lax.dot_general(a, b, (((0,), (0,)), ((), ())), preferred_element_type=F32)


def _sigmoid(x):
    return 1.0 / (1.0 + jnp.exp(-x))


def _softplus(x):
    return jnp.maximum(x, 0.0) + jnp.log1p(jnp.exp(-jnp.abs(x)))


def _prev_rows_seq(u, halo, j):
    tm = u.shape[0]
    return pltpu.roll(jnp.concatenate([u, halo], axis=0), j, 0)[0:tm]


def _state_rows_grouped(shifted, state, j, width):
    tm = shifted.shape[0]
    rmod = lax.broadcasted_iota(jnp.int32, shifted.shape, 0) & (SUBLANES - 1)
    back = width - 1 - j
    st = state if back == 0 else pltpu.roll(state, tm - back, 0)
    return jnp.where(rmod < j, st, shifted)


def _prev_rows_grouped(u, state, j, width):
    return _state_rows_grouped(pltpu.roll(u, j, 0), state, j, width)


def _mixer_a_kernel(x_ref, nw_ref, wb_ref, wc_ref, wx_ref, cw_ref, st_ref, u_ref, y_ref,
                    hs_ref, carry_ref, *, tiles_per_seq, grouped):
    i = pl.program_id(0)
    j = pl.program_id(1)

    @pl.when(j == 0)
    def _():
        hs_ref[...] = _rmsnorm(x_ref[...], nw_ref[...]).astype(BF16)

    hs = hs_ref[...]
    tm = hs.shape[0]
    for c in range(wb_ref.shape[1] // MXU_COLS):
        cols = slice(c * MXU_COLS, (c + 1) * MXU_COLS)
        gate_b = _dot(hs, wb_ref[:, cols])
        u = _dot(hs, wc_ref[:, cols]) * _dot(hs, wx_ref[:, cols])
        u_ref[:, cols] = u if grouped else u[tm - SUBLANES:tm]
        if grouped:
            state = st_ref[:, cols]
            prev1 = _prev_rows_grouped(u, state, 1, CONV_A_WIDTH)
            prev2 = _prev_rows_grouped(u, state, 2, CONV_A_WIDTH)
        else:
            halo = jnp.where(i % tiles_per_seq != 0, carry_ref[j, :, cols], 0.0)
            prev1 = _prev_rows_seq(u, halo, 1)
            prev2 = _prev_rows_seq(u, halo, 2)
            carry_ref[j, :, cols] = u[tm - SUBLANES:tm]
        y = prev2 * cw_ref[0:1, cols] + prev1 * cw_ref[1:2, cols] + u * cw_ref[2:3, cols]
        y_ref[:, cols] = (gate_b * y).astype(BF16)


def _mixer_a(x, norm_w, w_in, conv_w, state, *, seq_len, grouped):
    t = x.shape[0]
    tm, tn = ROW_TILE, 512
    nj = D_MODEL // tn
    if grouped:
        st_spec = pl.BlockSpec((tm, tn), lambda i, j: (i, j))
    else:
        st_spec = pl.BlockSpec((SUBLANES, tn), lambda i, j: (0, j))
    kern = functools.partial(_mixer_a_kernel, tiles_per_seq=max(seq_len // tm, 1), grouped=grouped)
    u_rows = tm if grouped else SUBLANES
    return pl.pallas_call(
        kern,
        grid=(t // tm, nj),
        in_specs=[
            pl.BlockSpec((tm, D_MODEL), lambda i, j: (i, 0)),
            pl.BlockSpec((1, D_MODEL), lambda i, j: (0, 0)),
            pl.BlockSpec((D_MODEL, tn), lambda i, j: (0, j)),
            pl.BlockSpec((D_MODEL, tn), lambda i, j: (0, nj + j)),
            pl.BlockSpec((D_MODEL, tn), lambda i, j: (0, 2 * nj + j)),
            pl.BlockSpec((CONV_A_WIDTH, tn), lambda i, j: (0, j)),
            st_spec,
        ],
        out_specs=[
            pl.BlockSpec((u_rows, tn), lambda i, j: (i, j)),
            pl.BlockSpec((tm, tn), lambda i, j: (i, j)),
        ],
        out_shape=[
            jax.ShapeDtypeStruct((t // tm * u_rows, D_MODEL), F32),
            jax.ShapeDtypeStruct((t, D_MODEL), BF16),
        ],
        scratch_shapes=[
            pltpu.VMEM((tm, D_MODEL), BF16),
            pltpu.VMEM((nj, SUBLANES, tn), F32),
        ],
        compiler_params=_params(("arbitrary", "arbitrary")),
        name="mixer_a",
    )(x, norm_w, w_in, w_in, w_in, conv_w, state)


def _matmul_residual_kernel(a_ref, w_ref, r_ref, o_ref):
    o_ref[...] = r_ref[...] + _dot(a_ref[...], w_ref[...])


def _matmul_residual(a, w, res):
    t, k = a.shape
    n = w.shape[1]
    tm, tn = ROW_TILE, 1024
    return pl.pallas_call(
        _matmul_residual_kernel,
        grid=(t // tm, n // tn),
        in_specs=[
            pl.BlockSpec((tm, k), lambda i, j: (i, 0)),
            pl.BlockSpec((k, tn), lambda i, j: (0, j)),
            pl.BlockSpec((tm, tn), lambda i, j: (i, j)),
        ],
        out_specs=pl.BlockSpec((tm, tn), lambda i, j: (i, j)),
        out_shape=jax.ShapeDtypeStruct((t, n), F32),
        compiler_params=_params(("parallel", "arbitrary")),
        name="matmul_residual",
    )(a, w, res)


def _mlp_kernel(x_ref, nw_ref, wu_ref, wd_ref, fw_ref, o_ref, hs_ref, *, final_norm):
    j = pl.program_id(1)

    @pl.when(j == 0)
    def _():
        x = x_ref[...]
        hs_ref[...] = _rmsnorm(x, nw_ref[...]).astype(BF16)
        o_ref[...] = x

    a = _dot(hs_ref[...], wu_ref[...])
    a = jnp.square(jnp.maximum(a, 0.0)).astype(BF16)
    o_ref[...] += _dot(a, wd_ref[...])

    if final_norm:
        @pl.when(j == pl.num_programs(1) - 1)
        def _():
            o_ref[...] = _rmsnorm(o_ref[...], fw_ref[...])


def _mlp(x, norm_w, w_up, w_down, final_w, *, layer, final_norm):
    t = x.shape[0]
    tm, tf = ROW_TILE, 512
    kern = functools.partial(_mlp_kernel, final_norm=final_norm)
    return pl.pallas_call(
        kern,
        grid=(t // tm, D_FF // tf),
        in_specs=[
            pl.BlockSpec((tm, D_MODEL), lambda i, j: (i, 0)),
            pl.BlockSpec((None, 1, D_MODEL), lambda i, j: (layer, 0, 0)),
            pl.BlockSpec((None, D_MODEL, tf), lambda i, j: (layer, 0, j)),
            pl.BlockSpec((None, tf, D_MODEL), lambda i, j: (layer, j, 0)),
            pl.BlockSpec((1, D_MODEL), lambda i, j: (0, 0)),
        ],
        out_specs=pl.BlockSpec((tm, D_MODEL), lambda i, j: (i, 0)),
        out_shape=jax.ShapeDtypeStruct((t, D_MODEL), F32),
        scratch_shapes=[pltpu.VMEM((tm, D_MODEL), BF16)],
        compiler_params=_params(("parallel", "arbitrary")),
        name="mlp",
    )(x, norm_w, w_up, w_down, final_w)


def _gdn_in_kernel(x_ref, nw_ref, w_ref, wg_ref, alog_ref, dtb_ref, qkvz_ref, gates_ref, hs_ref,
                   *, chunk):
    j = pl.program_id(1)

    @pl.when(j == 0)
    def _():
        hs = _rmsnorm(x_ref[...], nw_ref[...]).astype(BF16)
        hs_ref[...] = hs
        raw = _dot_nt(wg_ref[...], hs)
        tm = raw.shape[1]
        beta = _sigmoid(raw[0:V_HEADS])
        g = -jnp.exp(alog_ref[...]) * _softplus(raw[V_HEADS:2 * V_HEADS] + dtb_ref[...])
        pos = lax.broadcasted_iota(jnp.int32, g.shape, 1) & (chunk - 1)
        csum = g
        ssum = g
        s = 1
        while s < chunk:
            csum = csum + jnp.where(pos >= s, pltpu.roll(csum, s, 1), 0.0)
            ssum = ssum + jnp.where(pos < chunk - s, pltpu.roll(ssum, tm - s, 1), 0.0)
            s *= 2
        rest = ssum - g
        zero2 = jnp.zeros((2, tm), F32)
        for p in range(QK_HEADS):
            gates_ref[p, 0:2, :] = beta[2 * p:2 * p + 2]
            gates_ref[p, 2:4, :] = csum[2 * p:2 * p + 2]
            gates_ref[p, 4:6, :] = rest[2 * p:2 * p + 2]
            gates_ref[p, 6:8, :] = zero2

    qkvz_ref[...] = _dot_nt(hs_ref[...], w_ref[...])


def _gdn_in(x, norm_w, w_in_t, a_log_col, dt_bias_col, *, chunk):
    t = x.shape[0]
    tm, tn = ROW_TILE, 1536
    kern = functools.partial(_gdn_in_kernel, chunk=chunk)
    return pl.pallas_call(
        kern,
        grid=(t // tm, QKVZ_DIM // tn),
        in_specs=[
            pl.BlockSpec((tm, D_MODEL), lambda i, j: (i, 0)),
            pl.BlockSpec((1, D_MODEL), lambda i, j: (0, 0)),
            pl.BlockSpec((tn, D_MODEL), lambda i, j: (j, 0)),
            pl.BlockSpec((2 * V_HEADS, D_MODEL), lambda i, j: (QKVZ_DIM // (2 * V_HEADS), 0)),
            pl.BlockSpec((V_HEADS, 1), lambda i, j: (0, 0)),
            pl.BlockSpec((V_HEADS, 1), lambda i, j: (0, 0)),
        ],
        out_specs=[
            pl.BlockSpec((tm, tn), lambda i, j: (i, j)),
            pl.BlockSpec((QK_HEADS, SUBLANES, tm), lambda i, j: (0, 0, i)),
        ],
        out_shape=[
            jax.ShapeDtypeStruct((t, QKVZ_DIM), F32),
            jax.ShapeDtypeStruct((QK_HEADS, SUBLANES, t), F32),
        ],
        scratch_shapes=[pltpu.VMEM((tm, D_MODEL), BF16)],
        compiler_params=_params(("parallel", "arbitrary")),
        name="gdn_in",
    )(x, norm_w, w_in_t, w_in_t, a_log_col, dt_bias_col)


def _gdn_conv_kernel(x_ref, st_ref, cw_ref, o_ref, slab_ref, *, tiles_per_seq, grouped, tn):
    i = pl.program_id(0)
    j = pl.program_id(1)
    tm = x_ref.shape[0]
    n_qk = 2 * KEY_DIM // tn

    def run(normalise):
        scale = jnp.where(j < KEY_DIM // tn, HEAD_DIM ** -0.5, 1.0).astype(F32)
        for hd in range(tn // HEAD_DIM):
            cols = slice(hd * HEAD_DIM, (hd + 1) * HEAD_DIM)
            taps = [cw_ref[s:s + 1, cols] for s in range(CONV_B_WIDTH)]
            if grouped:
                slab_ref[0:SUBLANES, :] = jnp.zeros((SUBLANES, HEAD_DIM), F32)
            else:
                slab_ref[0:SUBLANES, :] = jnp.where(i % tiles_per_seq != 0, st_ref[:, cols], 0.0)
            slab_ref[SUBLANES:SUBLANES + tm, :] = x_ref[:, cols]

            def strip(rb, carry):
                r0 = pl.multiple_of(rb * CONV_STRIP, CONV_STRIP)
                u = slab_ref[pl.ds(r0 + SUBLANES, CONV_STRIP), :]
                prev = [slab_ref[pl.ds(r0 + SUBLANES - s, CONV_STRIP), :] for s in (1, 2, 3)]
                if grouped:
                    state = st_ref[pl.ds(r0, CONV_STRIP), cols]
                    prev = [_state_rows_grouped(prev[s - 1], state, s, CONV_B_WIDTH) for s in (1, 2, 3)]
                y = prev[2] * taps[0] + prev[1] * taps[1] + prev[0] * taps[2] + u * taps[3]
                o_ref[pl.ds(r0, CONV_STRIP), cols] = y * _sigmoid(y)
                return carry

            lax.fori_loop(0, tm // CONV_STRIP, strip, 0, unroll=True)
            if normalise:
                y = o_ref[:, cols]
                ss = jnp.sum(y * y, axis=-1, keepdims=True)
                o_ref[:, cols] = (y * lax.rsqrt(ss + EPS)) * scale

    @pl.when(j >= n_qk)
    def _():
        run(False)

    @pl.when(j < n_qk)
    def _():
        run(True)


def _gdn_conv(qkvz, state, conv_w, *, seq_len, grouped):
    t = qkvz.shape[0]
    tm, tn = CONV_TILE, 1024
    if grouped:
        st_spec = pl.BlockSpec((tm, tn), lambda i, j: (i, j))
        st = state
    else:
        blocks_per_tile = tm // SUBLANES
        st_spec = pl.BlockSpec((SUBLANES, tn), lambda i, j: (jnp.maximum(i * blocks_per_tile - 1, 0), j))
        st = qkvz
    kern = functools.partial(_gdn_conv_kernel, tiles_per_seq=max(seq_len // tm, 1), grouped=grouped, tn=tn)
    return pl.pallas_call(
        kern,
        grid=(t // tm, CONV_DIM // tn),
        in_specs=[
            pl.BlockSpec((tm, tn), lambda i, j: (i, j)),
            st_spec,
            pl.BlockSpec((CONV_B_WIDTH, tn), lambda i, j: (0, j)),
        ],
        out_specs=pl.BlockSpec((tm, tn), lambda i, j: (i, j)),
        out_shape=jax.ShapeDtypeStruct((t, CONV_DIM), F32),
        scratch_shapes=[pltpu.VMEM((SUBLANES + tm, HEAD_DIM), F32)],
        compiler_params=_params(("parallel", "parallel")),
        name="gdn_conv",
    )(qkvz, st, conv_w)


def _block_masks(rows, chunk):
    levels = chunk.bit_length() - 1
    ri = lax.broadcasted_iota(jnp.int32, (rows, rows), 0)
    ci = lax.broadcasted_iota(jnp.int32, (rows, rows), 1)
    same = [None] + [lax.shift_right_logical(ri, lb) == lax.shift_right_logical(ci, lb)
                     for lb in range(1, levels + 1)]
    return {
        "diag": ri == ci,
        "pair": same[1],
        "join": [same[lb + 1] & ~same[lb] for lb in range(1, levels)],
        "low": same[levels] & (ri >= ci),
        "strict": same[levels] & (ri > ci),
    }


def _gate_cols(g):
    rows = g.shape[1]
    padded = jnp.concatenate([g, jnp.zeros((HEAD_DIM - SUBLANES, rows), F32)], axis=0)
    return padded.T


def _gdn_chunk_parts(q_ref, k_ref, v_ref, gates, masks):
    heads = []
    for pp, grow in enumerate(gates):
        qcols = slice(pp * HEAD_DIM, (pp + 1) * HEAD_DIM)
        q = q_ref[:, qcols]
        k = k_ref[:, qcols]
        kb = k.astype(BF16)
        kk = _dot_nt(kb, kb)
        qk = _dot_nt(q.astype(BF16), kb)
        cols = _gate_cols(grow)
        for hh in range(2):
            hv = 2 * pp + hh
            beta_c = cols[:, hh:hh + 1]
            g_c = cols[:, 2 + hh:3 + hh]
            x_c = cols[:, 4 + hh:5 + hh]
            g_r = grow[2 + hh:3 + hh, :]
            decay = jnp.exp(jnp.where(masks["low"], g_c - g_r, -jnp.inf))
            eg = jnp.exp(g_c)
            v = v_ref[:, hv * HEAD_DIM:(hv + 1) * HEAD_DIM]
            heads.append({
                "l_mat": jnp.where(masks["strict"], beta_c * kk * decay, 0.0).astype(BF16),
                "rhs": jnp.concatenate([v * beta_c, k * (beta_c * eg)], axis=1).astype(BF16),
                "qk_m": (qk * decay).astype(BF16),
                "q_dec": q * eg,
                "k_dec": k * jnp.exp(x_c),
                "g_tot": g_c + x_c,
            })
    invs = [jnp.where(masks["diag"], 1.0, jnp.where(masks["pair"], -h["l_mat"], 0.0)) for h in heads]
    for join in masks["join"]:
        cross = [_dot(jnp.where(join, h["l_mat"], 0.0), inv).astype(BF16)
                 for h, inv in zip(heads, invs)]
        invs = [inv - _dot(inv, cr).astype(BF16) for inv, cr in zip(invs, cross)]
    for h, inv in zip(heads, invs):
        uw = _dot(inv, h["rhs"])
        h["u_mat"] = uw[:, :HEAD_DIM]
        h["w_mat"] = uw[:, HEAD_DIM:]
    return heads


def _gated_out(o, z, onorm):
    ms = jnp.mean(o * o, axis=-1, keepdims=True)
    return ((o * lax.rsqrt(ms + EPS)) * onorm) * (z * _sigmoid(z))


def _gdn_prompt_kernel(q_ref, k_ref, v_ref, z_ref, g_ref, on_ref, o_ref, s_ref, *, chunk, pairs):
    grp = pl.program_id(2)

    @pl.when(grp == 0)
    def _():
        s_ref[...] = jnp.zeros(s_ref.shape, F32)

    rows = q_ref.shape[0]
    masks = _block_masks(rows, chunk)
    heads = _gdn_chunk_parts(q_ref, k_ref, v_ref, [g_ref[pp] for pp in range(pairs)], masks)
    nh = len(heads)
    wbs = [h["w_mat"].astype(BF16) for h in heads]
    qdbs = [h["q_dec"].astype(BF16) for h in heads]
    kdbs = [h["k_dec"].astype(BF16) for h in heads]
    states = [s_ref[hv] for hv in range(nh)]
    v_new = [[] for _ in range(nh)]
    q_state = [[] for _ in range(nh)]
    for n in range(rows // chunk):
        rs = slice(n * chunk, (n + 1) * chunk)
        res = [_dot(jnp.concatenate([wb[rs], qdb[rs]], axis=0), st.astype(BF16))
               for wb, qdb, st in zip(wbs, qdbs, states)]
        for hv, h in enumerate(heads):
            v_n = h["u_mat"][rs] - res[hv][0:chunk]
            v_new[hv].append(v_n)
            q_state[hv].append(res[hv][chunk:2 * chunk])
            g_last = jnp.exp(h["g_tot"][n * chunk:n * chunk + 1, :])
            states[hv] = g_last * states[hv] + _dot_tn(kdbs[hv][rs], v_n.astype(BF16))
    onorm = on_ref[...]
    for hv, h in enumerate(heads):
        hcols = slice(hv * HEAD_DIM, (hv + 1) * HEAD_DIM)
        s_ref[hv] = states[hv]
        o = (jnp.concatenate(q_state[hv], axis=0)
             + _dot(h["qk_m"], jnp.concatenate(v_new[hv], axis=0).astype(BF16)))
        o_ref[:, hcols] = _gated_out(o, z_ref[:, hcols], onorm).astype(BF16)


def _gdn_prompt(qkv, qkvz, gates, o_norm, *, batch, seq_len):
    t = qkv.shape[0]
    r = PROMPT_GDN_ROWS
    groups = seq_len // r
    pairs = GDN_PAIRS
    hblocks = QK_HEADS // pairs
    qw, vw = pairs * HEAD_DIM, 2 * pairs * HEAD_DIM
    kern = functools.partial(_gdn_prompt_kernel, chunk=PROMPT_CHUNK, pairs=pairs)
    row = lambda b, h, g: b * groups + g
    return pl.pallas_call(
        kern,
        grid=(batch, hblocks, groups),
        in_specs=[
            pl.BlockSpec((r, qw), lambda b, h, g: (row(b, h, g), h)),
            pl.BlockSpec((r, qw), lambda b, h, g: (row(b, h, g), hblocks + h)),
            pl.BlockSpec((r, vw), lambda b, h, g: (row(b, h, g), hblocks + h)),
            pl.BlockSpec((r, vw), lambda b, h, g: (row(b, h, g), 2 * hblocks + h)),
            pl.BlockSpec((pairs, SUBLANES, r), lambda b, h, g: (h, 0, row(b, h, g))),
            pl.BlockSpec((1, HEAD_DIM), lambda b, h, g: (0, 0)),
        ],
        out_specs=[
            pl.BlockSpec((r, vw), lambda b, h, g: (row(b, h, g), h)),
            pl.BlockSpec((None, 2 * pairs, HEAD_DIM, HEAD_DIM), lambda b, h, g: (b, h, 0, 0)),
        ],
        out_shape=[
            jax.ShapeDtypeStruct((t, VAL_DIM), BF16),
            jax.ShapeDtypeStruct((batch, V_HEADS, HEAD_DIM, HEAD_DIM), F32),
        ],
        compiler_params=_params(("parallel", "parallel", "arbitrary")),
        name="gdn_prompt",
    )(qkv, qkv, qkv, qkvz, gates, o_norm)


def _gdn_sample_kernel(q_ref, k_ref, v_ref, z_ref, g_ref, on_ref, s0_ref, o_ref, s_ref,
                       lhs_scr, u_scr, kd_scr, gl_scr, vn_scr, os_scr, *, chunk):
    rows = q_ref.shape[0]
    nseq = rows // chunk
    masks = _block_masks(rows, chunk)
    heads = _gdn_chunk_parts(q_ref, k_ref, v_ref, [g_ref[...]], masks)
    for hh, h in enumerate(heads):
        lhs_scr[hh, :, 0:chunk, :] = h["w_mat"].reshape(nseq, chunk, HEAD_DIM)
        lhs_scr[hh, :, chunk:2 * chunk, :] = h["q_dec"].reshape(nseq, chunk, HEAD_DIM)
        u_scr[hh] = h["u_mat"]
        kd_scr[hh] = h["k_dec"]
        gl_scr[hh] = jnp.broadcast_to(jnp.exp(h["g_tot"]), (rows, HEAD_DIM))
    qk_masked = [h["qk_m"] for h in heads]
    onorm = on_ref[...]

    def per_sequences(step, carry):
        work = [(step * SEQ_UNROLL + lane, hh) for lane in range(SEQ_UNROLL) for hh in range(2)]
        rws = [pl.ds(pl.multiple_of(s * chunk, chunk), chunk) for s, _ in work]
        s0s = [s0_ref[s, hh] for s, hh in work]
        res = [_dot(lhs_scr[hh, s].astype(BF16), s0.astype(BF16))
               for (s, hh), s0 in zip(work, s0s)]
        v_new = [u_scr[hh, rw, :] - r[0:chunk] for (_, hh), rw, r in zip(work, rws, res)]
        upd = [_dot_tn(kd_scr[hh, rw, :], vn) for (_, hh), rw, vn in zip(work, rws, v_new)]
        for (s, hh), rw, s0, r, vn, up in zip(work, rws, s0s, res, v_new, upd):
            vn_scr[hh, rw, :] = vn
            os_scr[hh, rw, :] = r[chunk:2 * chunk]
            s_ref[s, hh] = gl_scr[hh, rw, :][0:1, :] * s0 + up
        return carry

    lax.fori_loop(0, nseq // SEQ_UNROLL, per_sequences, 0)
    for hh in range(2):
        hcols = slice(hh * HEAD_DIM, (hh + 1) * HEAD_DIM)
        o = os_scr[hh] + _dot(qk_masked[hh], vn_scr[hh].astype(BF16))
        o_ref[:, hcols] = _gated_out(o, z_ref[:, hcols], onorm).astype(BF16)


def _gdn_sample(qkv, qkvz, gates, o_norm, s0, *, chunk):
    t = qkv.shape[0]
    r = SAMPLE_GDN_ROWS
    nseq = r // chunk
    kern = functools.partial(_gdn_sample_kernel, chunk=chunk)
    return pl.pallas_call(
        kern,
        grid=(t // r, QK_HEADS),
        in_specs=[
            pl.BlockSpec((r, HEAD_DIM), lambda i, h: (i, h)),
            pl.BlockSpec((r, HEAD_DIM), lambda i, h: (i, QK_HEADS + h)),
            pl.BlockSpec((r, 2 * HEAD_DIM), lambda i, h: (i, QK_HEADS + h)),
            pl.BlockSpec((r, 2 * HEAD_DIM), lambda i, h: (i, 2 * QK_HEADS + h)),
            pl.BlockSpec((None, SUBLANES, r), lambda i, h: (h, 0, i)),
            pl.BlockSpec((1, HEAD_DIM), lambda i, h: (0, 0)),
            pl.BlockSpec((nseq, 2, HEAD_DIM, HEAD_DIM), lambda i, h: (i, h, 0, 0)),
        ],
        out_specs=[
            pl.BlockSpec((r, 2 * HEAD_DIM), lambda i, h: (i, h)),
            pl.BlockSpec((nseq, 2, HEAD_DIM, HEAD_DIM), lambda i, h: (i, h, 0, 0)),
        ],
        out_shape=[
            jax.ShapeDtypeStruct((t, VAL_DIM), BF16),
            jax.ShapeDtypeStruct(s0.shape, F32),
        ],
        scratch_shapes=[
            pltpu.VMEM((2, nseq, 2 * chunk, HEAD_DIM), F32),
            pltpu.VMEM((2, r, HEAD_DIM), F32),
            pltpu.VMEM((2, r, HEAD_DIM), F32),
            pltpu.VMEM((2, r, HEAD_DIM), F32),
            pltpu.VMEM((2, r, HEAD_DIM), F32),
            pltpu.VMEM((2, r, HEAD_DIM), F32),
        ],
        compiler_params=_params(("parallel", "parallel")),
        name="gdn_sample",
    )(qkv, qkv, qkv, qkvz, gates, o_norm, s0)


def _front_pad_rows(state, rows):
    b, w, c = state.shape
    return jnp.pad(state, ((0, 0), (0, rows - w), (0, 0))).reshape(b * rows, c)


def _trunk(x, conv_a, conv_b, ssm_b, wts, *, grouped):
    batch, seq_len, _ = x.shape
    t = batch * seq_len
    x0 = x.reshape(t, D_MODEL)
    if grouped:
        state_a = _front_pad_rows(conv_a, SUBLANES)
        state_b = _front_pad_rows(conv_b, SUBLANES)
        chunk = seq_len
    else:
        state_a = jnp.zeros((SUBLANES, D_MODEL), F32)
        state_b = None
        chunk = PROMPT_CHUNK

    u, y = _mixer_a(x0, wts["norm_a"], wts["w_in_a"], wts["w_conv_a"], state_a,
                    seq_len=seq_len, grouped=grouped)
    rows_kept = seq_len if grouped else u.shape[0] // batch
    new_conv_a = u.reshape(batch, rows_kept, D_MODEL)[:, rows_kept - (CONV_A_WIDTH - 1):]
    x1 = _matmul_residual(y, wts["w_out_a"], x0)
    x2 = _mlp(x1, wts["mlp_norm"], wts["w_up"], wts["w_down"], wts["final_norm"], layer=0,
              final_norm=False)

    qkvz, gates = _gdn_in(x2, wts["norm_b"], wts["w_in_b_t"], wts["a_log"], wts["dt_bias"],
                          chunk=chunk)
    new_conv_b = qkvz.reshape(batch, seq_len, QKVZ_DIM)[:, seq_len - (CONV_B_WIDTH - 1):, :CONV_DIM]
    qkv = _gdn_conv(qkvz, state_b, wts["w_conv_b"], seq_len=seq_len, grouped=grouped)
    if grouped:
        o, s_new = _gdn_sample(qkv, qkvz, gates, wts["o_norm"], ssm_b, chunk=chunk)
    else:
        o, s_new = _gdn_prompt(qkv, qkvz, gates, wts["o_norm"], batch=batch, seq_len=seq_len)
    x3 = _matmul_residual(o, wts["w_out_b"], x2)
    x4 = _mlp(x3, wts["mlp_norm"], wts["w_up"], wts["w_down"], wts["final_norm"], layer=1,
              final_norm=True)
    return (x4.reshape(batch, seq_len, D_MODEL), new_conv_a[None], new_conv_b[None], s_new[None])


def kernel(x_prompt, x_sample, state_conv_a, state_conv_b, state_ssm_b, norm_a, w_in_a, w_conv_a,
           w_out_a, norm_b, w_in_b, w_conv_b, a_log_b, dt_bias_b, o_norm_b, w_out_b, mlp_norm,
           w_up, w_down, final_norm):
    wts = {
        "norm_a": norm_a[0][None],
        "w_in_a": w_in_a[0].astype(BF16),
        "w_conv_a": w_conv_a[0],
        "w_out_a": w_out_a[0].astype(BF16),
        "norm_b": norm_b[0][None],
        "w_in_b_t": w_in_b[0].T.astype(BF16),
        "w_conv_b": w_conv_b[0],
        "a_log": a_log_b[0][:, None],
        "dt_bias": dt_bias_b[0][:, None],
        "o_norm": o_norm_b[0][None],
        "w_out_b": w_out_b[0].astype(BF16),
        "mlp_norm": mlp_norm[:, None, :],
        "w_up": w_up.astype(BF16),
        "w_down": w_down.astype(BF16),
        "final_norm": final_norm[None],
    }
    y_p, ca_p, cb_p, s_p = _trunk(x_prompt, None, None, None, wts, grouped=False)
    y_s, ca_s, cb_s, s_s = _trunk(x_sample, state_conv_a[0], state_conv_b[0], state_ssm_b[0], wts,
                                  grouped=True)
    return (y_p, y_s, ca_p, cb_p, s_p, ca_s, cb_s, s_s)
```
